```python
import jax, jax.numpy as jnp
from jax import lax
import numpy as np

D_MODEL = 2048
BATCH = 4
SEQ = 2048
DEPTH = 4
DEC_BATCH = 128
DEC_SEQ = 8
PAST_LEN = 16384
PAGE_SIZE = 128

N_MIXERS = 3
SSM_GROUP = 16
SSM_GROUPS = D_MODEL // SSM_GROUP
SSM_STATE = 64
DT_MIN = 1e-3
DT_MAX = 1e-1
CF_WIDTH = 31
SC_WIDTH = 3
N_EXPERT_GROUPS = 8
EXPERTS_PER_GROUP = 8
N_EXPERTS = N_EXPERT_GROUPS * EXPERTS_PER_GROUP
TOP_K = 2
D_EXPERT = D_MODEL // 4
MOE_BLOCK = 64
EPS = 1e-6

N_A_LAYERS = (DEPTH + 2) // 3
N_B_LAYERS = (DEPTH + 1) // 3
N_C_LAYERS = DEPTH // 3

kernel_name = 'hybrid_s5_conformer_shortconv_hmoe_step'


def rmsnorm(x, g):
    xf = x.astype(jnp.float32)
    y = xf * lax.rsqrt(jnp.mean(xf * xf, axis=-1, keepdims=True) + EPS)
    return (y * g.astype(jnp.float32)).astype(x.dtype)


def layernorm(x, g, b):
    xf = x.astype(jnp.float32)
    mu = jnp.mean(xf, axis=-1, keepdims=True)
    xc = xf - mu
    y = xc * lax.rsqrt(jnp.mean(xc * xc, axis=-1, keepdims=True) + EPS)
    return (y * g.astype(jnp.float32) + b.astype(jnp.float32)).astype(x.dtype)


def modulate(x, g, shift, scale):
    return rmsnorm(x, g) * (1 + scale[:, None, :]) + shift[:, None, :]


def causal_dwconv(buf, v, w):
    full = jnp.concatenate([buf.astype(v.dtype), v], axis=1)
    out = lax.conv_general_dilated(full, w[:, None, :].astype(v.dtype), window_strides=(1,), padding='VALID',
                                   dimension_numbers=('NWC', 'WIO', 'NWC'), feature_group_count=v.shape[-1])
    return out, full[:, -(w.shape[0] - 1):]


def s5_discretize(a_re, a_im, log_dt, b_re, b_im):
    a_re = a_re.astype(jnp.float32)
    a_im = a_im.astype(jnp.float32)
    dt = jnp.exp(log_dt.astype(jnp.float32))[:, None]
    mag = jnp.exp(dt * a_re)
    ang = dt * a_im
    ab_re = mag * jnp.cos(ang)
    ab_im = mag * jnp.sin(ang)
    n_re = ab_re - 1.0
    n_im = ab_im
    den = a_re * a_re + a_im * a_im
    f_re = ((n_re * a_re + n_im * a_im) / den)[..., None]
    f_im = ((n_im * a_re - n_re * a_im) / den)[..., None]
    b_re = b_re.astype(jnp.float32)
    b_im = b_im.astype(jnp.float32)
    bb_re = f_re * b_re - f_im * b_im
    bb_im = f_re * b_im + f_im * b_re
    return ab_re, ab_im, bb_re, bb_im


def _complex_affine_combine(e1, e2):
    a1r, a1i, b1r, b1i = e1
    a2r, a2i, b2r, b2i = e2
    return (a2r * a1r - a2i * a1i, a2r * a1i + a2i * a1r,
            a2r * b1r - a2i * b1i + b2r, a2r * b1i + a2i * b1r + b2i)


def s5_mixer(u, h0_re, h0_im, a_re, a_im, log_dt, b_re, b_im, c_re, c_im, d_skip, w_glu):
    bsz, seq, d = u.shape
    uf = u.astype(jnp.float32).reshape(bsz, seq, SSM_GROUPS, SSM_GROUP)
    ab_re, ab_im, bb_re, bb_im = s5_discretize(a_re, a_im, log_dt, b_re, b_im)
    bu_re = jnp.einsum('blgc,gpc->lbgp', uf, bb_re)
    bu_im = jnp.einsum('blgc,gpc->lbgp', uf, bb_im)
    h0r = h0_re.astype(jnp.float32)
    h0i = h0_im.astype(jnp.float32)
    bu_re = bu_re.at[0].add(ab_re * h0r - ab_im * h0i)
    bu_im = bu_im.at[0].add(ab_re * h0i + ab_im * h0r)
    a_sr = jnp.broadcast_to(ab_re, (seq, 1, SSM_GROUPS, SSM_STATE))
    a_si = jnp.broadcast_to(ab_im, (seq, 1, SSM_GROUPS, SSM_STATE))
    _, _, h_re, h_im = lax.associative_scan(_complex_affine_combine, (a_sr, a_si, bu_re, bu_im), axis=0)
    y = (jnp.einsum('lbgp,gcp->blgc', h_re, c_re.astype(jnp.float32))
         - jnp.einsum('lbgp,gcp->blgc', h_im, c_im.astype(jnp.float32))).reshape(bsz, seq, d)
    y = (y + d_skip.astype(jnp.float32) * u.astype(jnp.float32)).astype(u.dtype)
    z = jax.nn.gelu(y) @ w_glu
    za, zb = jnp.split(z, 2, axis=-1)
    return za * jax.nn.sigmoid(zb), h_re[-1].astype(h0_re.dtype), h_im[-1].astype(h0_im.dtype)


def conformer_conv_mixer(u, buf, w_pw1, w_dw, b_dw, ln_g, ln_b, w_pw2):
    z = u @ w_pw1
    za, zb = jnp.split(z, 2, axis=-1)
    g = za * jax.nn.sigmoid(zb)
    y, new_buf = causal_dwconv(buf, g, w_dw)
    y = layernorm(y + b_dw, ln_g, ln_b)
    return jax.nn.silu(y) @ w_pw2, new_buf


def short_conv_mixer(u, buf, w_in, w_conv, w_out):
    z = u @ w_in
    bg, cg, hv = jnp.split(z, 3, axis=-1)
    y, new_buf = causal_dwconv(buf, cg * hv, w_conv)
    return (bg * y) @ w_out, new_buf


def moe_ffn(h, w_group, b_group, w_expert, b_expert, w13, w2):
    n_tok, d = h.shape
    hf = h.astype(jnp.float32)
    g_logits = hf @ w_group.astype(jnp.float32) + b_group.astype(jnp.float32)
    grp = jnp.argmax(g_logits, axis=-1).astype(jnp.int32)
    p_grp = jnp.take_along_axis(jax.nn.softmax(g_logits, axis=-1), grp[:, None], axis=1)[:, 0]
    e_logits = (hf @ w_expert.astype(jnp.float32) + b_expert.astype(jnp.float32)).reshape(
        n_tok, N_EXPERT_GROUPS, EXPERTS_PER_GROUP)
    e_in = jnp.take_along_axis(e_logits, grp[:, None, None], axis=1)[:, 0]
    top_v, top_i = lax.top_k(e_in, TOP_K)
    gate = jax.nn.softmax(top_v, axis=-1) * p_grp[:, None]
    expert = grp[:, None] * EXPERTS_PER_GROUP + top_i.astype(jnp.int32)
    n_asg = n_tok * TOP_K
    flat_e = expert.reshape(n_asg)
    flat_t = jnp.repeat(jnp.arange(n_tok, dtype=jnp.int32), TOP_K)
    flat_w = gate.reshape(n_asg)
    order = jnp.argsort(flat_e)
    se = flat_e[order]
    counts = jnp.bincount(flat_e, length=N_EXPERTS)
    starts = jnp.cumsum(counts) - counts
    padded = (counts + MOE_BLOCK - 1) // MOE_BLOCK * MOE_BLOCK
    pends = jnp.cumsum(padded)
    pstarts = pends - padded
    dest = pstarts[se] + (jnp.arange(n_asg, dtype=jnp.int32) - starts[se])
    n_blocks = (n_asg + N_EXPERTS * (MOE_BLOCK - 1) + MOE_BLOCK - 1) // MOE_BLOCK
    n_rows = n_blocks * MOE_BLOCK
    row_tok = jnp.full((n_rows,), n_tok, dtype=jnp.int32).at[dest].set(flat_t[order])
    row_w = jnp.zeros((n_rows,), dtype=h.dtype).at[dest].set(flat_w[order].astype(h.dtype))
    blk_e = jnp.minimum(jnp.searchsorted(pends, jnp.arange(n_blocks, dtype=jnp.int32) * MOE_BLOCK, side='right'),
                        N_EXPERTS - 1).astype(jnp.int32)
    x_pad = jnp.concatenate([h, jnp.zeros((1, d), dtype=h.dtype)], axis=0)
    xb = x_pad[row_tok].reshape(n_blocks, MOE_BLOCK, d)

    def expert_block(args):
        xblk, e = args
        ab = xblk @ w13[e]
        a, b = jnp.split(ab, 2, axis=-1)
        return (jax.nn.silu(a) * b) @ w2[e]

    yb = lax.map(expert_block, (xb, blk_e)).reshape(n_rows, d)
    out = jnp.zeros((n_tok + 1, d), dtype=h.dtype).at[row_tok].add(yb * row_w[:, None])
    return out[:n_tok]


def setup_inputs(seed: int = 0) -> dict:
    key = jax.random.key(seed)
    ks = list(jax.random.split(key, 48))
    nk = lambda: ks.pop()
    f32 = jnp.float32
    D = D_MODEL
    G, P, C = SSM_GROUPS, SSM_STATE, SSM_GROUP
    nrm = lambda shape, s: jax.random.normal(nk(), shape, f32) * s
    inp = {}
    inp['x_prompt'] = nrm((BATCH, SEQ, D), 1.0)
    inp['x_sample'] = nrm((DEC_BATCH, DEC_SEQ, D), 1.0)
    inp['c_prompt'] = nrm((BATCH, D), 1.0)
    inp['c_sample'] = nrm((DEC_BATCH, D), 1.0)
    inp['state_l0_ssm_re'] = nrm((DEC_BATCH, G, P), 0.1)
    inp['state_l0_ssm_im'] = nrm((DEC_BATCH, G, P), 0.1)
    inp['cache_l1_conformer_conv'] = nrm((DEC_BATCH, CF_WIDTH - 1, D), 0.5)
    inp['cache_l2_short_conv'] = nrm((DEC_BATCH, SC_WIDTH - 1, D), 0.5)
    inp['state_l3_ssm_re'] = nrm((DEC_BATCH, G, P), 0.1)
    inp['state_l3_ssm_im'] = nrm((DEC_BATCH, G, P), 0.1)
    inp['norm1_g'] = 1.0 + nrm((DEPTH, D), 0.02)
    inp['norm2_g'] = 1.0 + nrm((DEPTH, D), 0.02)
    inp['w_ada'] = nrm((DEPTH, D, 6 * D), 0.5 * D ** -0.5)
    inp['b_ada'] = nrm((DEPTH, 6 * D), 0.02)
    inp['final_norm_g'] = 1.0 + nrm((D,), 0.02)
    inp['ssm_a_re'] = -0.5 + nrm((N_A_LAYERS, G, P), 0.01)
    inp['ssm_a_im'] = jnp.pi * jnp.arange(P, dtype=f32)[None, None, :] + nrm((N_A_LAYERS, G, P), 0.01)
    inp['ssm_log_dt'] = jax.random.uniform(nk(), (N_A_LAYERS, G), f32, np.log(DT_MIN), np.log(DT_MAX))
    inp['ssm_b_re'] = nrm((N_A_LAYERS, G, P, C), (2 * C) ** -0.5)
    inp['ssm_b_im'] = nrm((N_A_LAYERS, G, P, C), (2 * C) ** -0.5)
    inp['ssm_c_re'] = nrm((N_A_LAYERS, G, C, P), P ** -0.5)
    inp['ssm_c_im'] = nrm((N_A_LAYERS, G, C, P), P ** -0.5)
    inp['ssm_d'] = nrm((N_A_LAYERS, D), 1.0)
    inp['ssm_w_glu'] = nrm((N_A_LAYERS, D, 2 * D), D ** -0.5)
    inp['cf_w_pw1'] = nrm((N_B_LAYERS, D, 2 * D), D ** -0.5)
    inp['cf_w_dw'] = nrm((N_B_LAYERS, CF_WIDTH, D), CF_WIDTH ** -0.5)
    inp['cf_b_dw'] = nrm((N_B_LAYERS, D), 0.01)
    inp['cf_ln_g'] = 1.0 + nrm((N_B_LAYERS, D), 0.02)
    inp['cf_ln_b'] = nrm((N_B_LAYERS, D), 0.01)
    inp['cf_w_pw2'] = nrm((N_B_LAYERS, D, D), D ** -0.5)
    inp['sc_w_in'] = nrm((N_C_LAYERS, D, 3 * D), D ** -0.5)
    inp['sc_w_conv'] = nrm((N_C_LAYERS, SC_WIDTH, D), SC_WIDTH ** -0.5)
    inp['sc_w_out'] = nrm((N_C_LAYERS, D, D), D ** -0.5)
    inp['moe_w_group'] = nrm((DEPTH, D, N_EXPERT_GROUPS), D ** -0.5)
    inp['moe_b_group'] = nrm((DEPTH, N_EXPERT_GROUPS), 0.01)
    inp['moe_w_expert'] = nrm((DEPTH, D, N_EXPERTS), D ** -0.5)
    inp['moe_b_expert'] = nrm((DEPTH, N_EXPERTS), 0.01)
    inp['moe_w13'] = nrm((DEPTH, N_EXPERTS, D, 2 * D_EXPERT), D ** -0.5)
    inp['moe_w2'] = nrm((DEPTH, N_EXPERTS, D_EXPERT, D), D_EXPERT ** -0.5)
    return inp


def reference(x_prompt, x_sample, c_prompt, c_sample,
              state_l0_ssm_re, state_l0_ssm_im, cache_l1_conformer_conv, cache_l2_short_conv,
              state_l3_ssm_re, state_l3_ssm_im,
              norm1_g, norm2_g, w_ada, b_ada, final_norm_g,
              ssm_a_re, ssm_a_im, ssm_log_dt, ssm_b_re, ssm_b_im, ssm_c_re, ssm_c_im, ssm_d, ssm_w_glu,
              cf_w_pw1, cf_w_dw, cf_b_dw, cf_ln_g, cf_ln_b, cf_w_pw2,
              sc_w_in, sc_w_conv, sc_w_out,
              moe_w_group, moe_b_group, moe_w_expert, moe_b_expert, moe_w13, moe_w2):

    def trunk(x, c, init_states):
        ada_in = jax.nn.silu(c)
        new_states = []
        for i in range(DEPTH):
            ada = ada_in @ w_ada[i] + b_ada[i]
            sh1, sc1, g1, sh2, sc2, g2 = jnp.split(ada, 6, axis=-1)
            h = modulate(x, norm1_g[i], sh1, sc1)
            kind, j = i % N_MIXERS, i // N_MIXERS
            if kind == 0:
                m, s_re, s_im = s5_mixer(h, init_states[i][0], init_states[i][1], ssm_a_re[j], ssm_a_im[j],
                                         ssm_log_dt[j], ssm_b_re[j], ssm_b_im[j], ssm_c_re[j], ssm_c_im[j],
                                         ssm_d[j], ssm_w_glu[j])
                new_states.append((s_re, s_im))
            elif kind == 1:
                m, buf = conformer_conv_mixer(h, init_states[i][0], cf_w_pw1[j], cf_w_dw[j], cf_b_dw[j],
                                              cf_ln_g[j], cf_ln_b[j], cf_w_pw2[j])
                new_states.append((buf,))
            else:
                m, buf = short_conv_mixer(h, init_states[i][0], sc_w_in[j], sc_w_conv[j], sc_w_out[j])
                new_states.append((buf,))
            x = x + g1[:, None, :] * m
            h = modulate(x, norm2_g[i], sh2, sc2)
            bsz, seq, d = h.shape
            f = moe_ffn(h.reshape(bsz * seq, d), moe_w_group[i], moe_b_group[i], moe_w_expert[i],
                        moe_b_expert[i], moe_w13[i], moe_w2[i]).reshape(bsz, seq, d)
            x = x + g2[:, None, :] * f
        return rmsnorm(x, final_norm_g), new_states

    bp = x_prompt.shape[0]
    prompt_init = []
    for i in range(DEPTH):
        kind = i % N_MIXERS
        if kind == 0:
            z = jnp.zeros((bp, SSM_GROUPS, SSM_STATE), dtype=jnp.float32)
            prompt_init.append((z, z))
        elif kind == 1:
            prompt_init.append((jnp.zeros((bp, CF_WIDTH - 1, D_MODEL), dtype=x_prompt.dtype),))
        else:
            prompt_init.append((jnp.zeros((bp, SC_WIDTH - 1, D_MODEL), dtype=x_prompt.dtype),))
    sample_init = [(state_l0_ssm_re, state_l0_ssm_im), (cache_l1_conformer_conv,), (cache_l2_short_conv,),
                   (state_l3_ssm_re, state_l3_ssm_im)]

    y_prompt, ps = trunk(x_prompt, c_prompt, prompt_init)
    y_sample, ss = trunk(x_sample, c_sample, sample_init)
    return (y_prompt, y_sample,
            ps[0][0], ps[0][1], ps[1][0], ps[2][0], ps[3][0], ps[3][1],
            ss[0][0], ss[0][1], ss[1][0], ss[2][0], ss[3][0], ss[3][1])
```

```python
import functools

import jax
import jax.numpy as jnp
from jax import lax
from jax.experimental import pallas as pl
from jax.experimental.pallas import tpu as pltpu

F32 = jnp.float32
BF16 = jnp.bfloat16

D_MODEL = 2048
SSM_GROUP = 16
SSM_STATE = 64
SSM_GB = 16
SSM_CH = SSM_GB * SSM_GROUP
SSM_ST = SSM_GB * SSM_STATE
CF_WIDTH = 31
SC_WIDTH = 3
N_GROUPS = 8
EPG = 8
N_EXPERTS = N_GROUPS * EPG
D_EXPERT = D_MODEL // 4
EPS = 1e-6

TM = 512
MOE_BM = 128
TMC = 256
ROUTE_LANES = 128
VMEM_LIMIT = 56 * 1024 * 1024


def _cparams(sem):
    return pltpu.CompilerParams(dimension_semantics=sem, vmem_limit_bytes=VMEM_LIMIT)


def _dot(a, b):
    return jnp.dot(a, b, preferred_element_type=F32)


def _dotb(a, b):
    return _dot(a.astype(BF16), b.astype(BF16))


def _sigmoid(x):
    return 1.0 / (1.0 + jnp.exp(-x))


def _silu(x):
    return x * _sigmoid(x)


def _gelu_tanh(x):
    return 0.5 * x * (1.0 + jnp.tanh(0.7978845608028654 * (x + 0.044715 * (x * x * x))))


def _seq_vec(vp_ref, vs_ref, blk, n_prompt_blocks, blocks_per_batch, rows):
    b = jnp.minimum(blk // blocks_per_batch, vp_ref.shape[0] - 1)
    vp = vp_ref[pl.ds(b, 1), :]
    vs = vs_ref[...]
    reps = rows // vs.shape[0]
    if reps > 1:
        vs = jnp.concatenate([vs] * reps, axis=0)
    return jnp.where(blk < n_prompt_blocks, vp, vs)


def _ada_kernel(c_ref, w_ref, b_ref, o_ref):
    c = c_ref[...]
    o_ref[0] = _dotb(_silu(c), w_ref[0]) + b_ref[0]


def _ada_all(c_all, w_ada, b_ada):
    depth, d, n = w_ada.shape
    rows = c_all.shape[0]
    tn = 1024
    return pl.pallas_call(
        _ada_kernel,
        out_shape=jax.ShapeDtypeStruct((depth, rows, n), F32),
        grid=(depth, n // tn),
        in_specs=[
            pl.BlockSpec((rows, d), lambda l, j: (0, 0)),
            pl.BlockSpec((1, d, tn), lambda l, j: (l, 0, j)),
            pl.BlockSpec((1, 1, tn), lambda l, j: (l, 0, j)),
        ],
        out_specs=pl.BlockSpec((1, rows, tn), lambda l, j: (l, 0, j)),
        compiler_params=_cparams(("parallel", "parallel")),
        name="ada_proj",
    )(c_all, w_ada, b_ada.reshape(depth, 1, n))


def _rms(x, g):
    return x * lax.rsqrt(jnp.mean(x * x, axis=-1, keepdims=True) + EPS) * g


def _modulate_kernel(x_ref, g_ref, shp_ref, scp_ref, shs_ref, scs_ref, o_ref, *, npb, bpb):
    i = pl.program_id(0)
    rows = x_ref.shape[0]
    y = _rms(x_ref[...], g_ref[...])
    sc = _seq_vec(scp_ref, scs_ref, i, npb, bpb, rows)
    sh = _seq_vec(shp_ref, shs_ref, i, npb, bpb, rows)
    o_ref[...] = (y * (1.0 + sc) + sh).astype(o_ref.dtype)


def _route_from_logits(lg):
    rows = lg.shape[0]
    lane = lax.broadcasted_iota(jnp.int32, (rows, ROUTE_LANES), 1)
    lane_f = lane.astype(F32)
    neg = jnp.float32(-jnp.inf)
    big = jnp.float32(ROUTE_LANES)
    is_g = lane < N_GROUPS
    gl = jnp.where(is_g, lg, neg)
    gmax = jnp.max(gl, axis=1, keepdims=True)
    grp = jnp.min(jnp.where(gl == gmax, lane_f, big), axis=1, keepdims=True)
    p_grp = 1.0 / jnp.sum(jnp.where(is_g, jnp.exp(gl - gmax), 0.0), axis=1, keepdims=True)
    e_grp = ((lane - N_GROUPS) >> 3).astype(F32)
    valid = (lane >= N_GROUPS) & (lane < N_GROUPS + N_EXPERTS) & (e_grp == grp)
    el = jnp.where(valid, lg, neg)
    v1 = jnp.max(el, axis=1, keepdims=True)
    i1 = jnp.min(jnp.where(el == v1, lane_f, big), axis=1, keepdims=True)
    el2 = jnp.where(lane_f == i1, neg, el)
    v2 = jnp.max(el2, axis=1, keepdims=True)
    i2 = jnp.min(jnp.where(el2 == v2, lane_f, big), axis=1, keepdims=True)
    t = jnp.exp(v2 - v1)
    den = 1.0 + t
    w1 = (1.0 / den) * p_grp
    w2 = (t / den) * p_grp
    out = jnp.where(lane == 0, i1 - N_GROUPS, 0.0)
    out = jnp.where(lane == 1, i2 - N_GROUPS, out)
    out = jnp.where(lane == 2, w1, out)
    out = jnp.where(lane == 3, w2, out)
    return out


def _modulate_route_kernel(x_ref, g_ref, shp_ref, scp_ref, shs_ref, scs_ref, wr_ref, br_ref,
                           o_ref, r_ref, *, npb, bpb):
    i = pl.program_id(0)
    rows = x_ref.shape[0]
    y = _rms(x_ref[...], g_ref[...])
    sc = _seq_vec(scp_ref, scs_ref, i, npb, bpb, rows)
    sh = _seq_vec(shp_ref, shs_ref, i, npb, bpb, rows)
    h = y * (1.0 + sc) + sh
    o_ref[...] = h
    lg = _dotb(h, wr_ref[...]) + br_ref[...]
    r_ref[...] = _route_from_logits(lg)


def _modulate(x, g, ap, as_, sh_idx, sc_idx, mp, lp, out_dtype, router=None):
    m, d = x.shape
    bs = as_.shape[0]
    npb, bpb = mp // TM, lp // TM
    in_specs = [
        pl.BlockSpec((TM, d), lambda i: (i, 0)),
        pl.BlockSpec((1, d), lambda i: (0, 0)),
        pl.BlockSpec((8, d), lambda i: (0, sh_idx)),
        pl.BlockSpec((8, d), lambda i: (0, sc_idx)),
        pl.BlockSpec((bs, d), lambda i: (0, sh_idx)),
        pl.BlockSpec((bs, d), lambda i: (0, sc_idx)),
    ]
    if router is None:
        return pl.pallas_call(
            functools.partial(_modulate_kernel, npb=npb, bpb=bpb),
            out_shape=jax.ShapeDtypeStruct((m, d), out_dtype),
            grid=(m // TM,),
            in_specs=in_specs,
            out_specs=pl.BlockSpec((TM, d), lambda i: (i, 0)),
            compiler_params=_cparams(("parallel",)),
            name="modulate",
        )(x, g, ap, ap, as_, as_)
    wr, br = router
    return pl.pallas_call(
        functools.partial(_modulate_route_kernel, npb=npb, bpb=bpb),
        out_shape=(jax.ShapeDtypeStruct((m, d), F32), jax.ShapeDtypeStruct((m, ROUTE_LANES), F32)),
        grid=(m // TM,),
        in_specs=in_specs + [
            pl.BlockSpec((d, ROUTE_LANES), lambda i: (0, 0)),
            pl.BlockSpec((1, ROUTE_LANES), lambda i: (0, 0)),
        ],
        out_specs=(pl.BlockSpec((TM, d), lambda i: (i, 0)), pl.BlockSpec((TM, ROUTE_LANES), lambda i: (i, 0))),
        compiler_params=_cparams(("parallel",)),
        name="modulate_route",
    )(x, g, ap, ap, as_, as_, wr, br)


def _final_norm_kernel(x_ref, g_ref, o_ref):
    o_ref[...] = _rms(x_ref[...], g_ref[...])


def _final_norm(x, g):
    m, d = x.shape
    return pl.pallas_call(
        _final_norm_kernel,
        out_shape=jax.ShapeDtypeStruct((m, d), F32),
        grid=(m // TM,),
        in_specs=[pl.BlockSpec((TM, d), lambda i: (i, 0)), pl.BlockSpec((1, d), lambda i: (0, 0))],
        out_specs=pl.BlockSpec((TM, d), lambda i: (i, 0)),
        compiler_params=_cparams(("parallel",)),
        name="final_norm",
    )(x, g)


def _mm_products(a_ref, w_refs, wbf_ref):
    @pl.when(pl.program_id(1) == 0)
    def _():
        for s, w_ref in enumerate(w_refs):
            wbf_ref[s] = w_ref[...].astype(BF16)

    a = a_ref[...].astype(BF16)
    return [_dot(a, wbf_ref[s]) for s in range(len(w_refs))]


def _mm_glu_kernel(a_ref, wa_ref, wb_ref, o_ref, wbf_ref):
    za, zb = _mm_products(a_ref, (wa_ref, wb_ref), wbf_ref)
    o_ref[...] = za * _sigmoid(zb)


def _mm_bcv_kernel(a_ref, w0_ref, w1_ref, w2_ref, bg_ref, v_ref, wbf_ref):
    bg, cg, hv = _mm_products(a_ref, (w0_ref, w1_ref, w2_ref), wbf_ref)
    bg_ref[...] = bg
    v_ref[...] = cg * hv


def _mm_resid_kernel(a_ref, w_ref, x_ref, gp_ref, gs_ref, o_ref, wbf_ref, *, npb, bpb):
    (z,) = _mm_products(a_ref, (w_ref,), wbf_ref)
    i = pl.program_id(1)
    gate = _seq_vec(gp_ref, gs_ref, i, npb, bpb, x_ref.shape[0])
    o_ref[...] = x_ref[...] + gate * z


def _w_spec(k, tn, layer, col0):
    return pl.BlockSpec((None, k, tn), lambda j, i: (layer, 0, col0 + j))


def _mm_split(a, w, layer, n_split, kernel_fn, n_out, tn, name):
    m, k = a.shape
    n = w.shape[2] // n_split
    nj = n // tn
    out_shape = [jax.ShapeDtypeStruct((m, n), F32) for _ in range(n_out)]
    out_specs = [pl.BlockSpec((TM, tn), lambda j, i: (i, j)) for _ in range(n_out)]
    return pl.pallas_call(
        kernel_fn,
        out_shape=out_shape if n_out > 1 else out_shape[0],
        grid=(nj, m // TM),
        in_specs=[pl.BlockSpec((TM, k), lambda j, i: (i, 0))]
        + [_w_spec(k, tn, layer, s * nj) for s in range(n_split)],
        out_specs=out_specs if n_out > 1 else out_specs[0],
        scratch_shapes=[pltpu.VMEM((n_split, k, tn), BF16)],
        compiler_params=_cparams(("parallel", "arbitrary")),
        name=name,
    )(a, *([w] * n_split))


def _mm_glu_resid_kernel(a_ref, wa_ref, wb_ref, x_ref, gp_ref, gs_ref, o_ref, wbf_ref, *, npb, bpb):
    za, zb = _mm_products(a_ref, (wa_ref, wb_ref), wbf_ref)
    i = pl.program_id(1)
    gate = _seq_vec(gp_ref, gs_ref, i, npb, bpb, x_ref.shape[0])
    o_ref[...] = x_ref[...] + gate * (za * _sigmoid(zb))


def _mm_resid(a, w, layer, x, ap, as_, gate_idx, mp, lp, glu=False):
    m, k = a.shape
    n_split = 2 if glu else 1
    n = w.shape[2] // n_split
    tn = 512 if glu else 1024
    nj = n // tn
    bs = as_.shape[0]
    kernel_fn = _mm_glu_resid_kernel if glu else _mm_resid_kernel
    return pl.pallas_call(
        functools.partial(kernel_fn, npb=mp // TM, bpb=lp // TM),
        out_shape=jax.ShapeDtypeStruct((m, n), F32),
        grid=(nj, m // TM),
        in_specs=[pl.BlockSpec((TM, k), lambda j, i: (i, 0))]
        + [_w_spec(k, tn, layer, s * nj) for s in range(n_split)]
        + [
            pl.BlockSpec((TM, tn), lambda j, i: (i, j)),
            pl.BlockSpec((8, tn), lambda j, i: (0, gate_idx * nj + j)),
            pl.BlockSpec((bs, tn), lambda j, i: (0, gate_idx * nj + j)),
        ],
        out_specs=pl.BlockSpec((TM, tn), lambda j, i: (i, j)),
        scratch_shapes=[pltpu.VMEM((n_split, k, tn), BF16)],
        compiler_params=_cparams(("parallel", "arbitrary")),
        name="mm_glu_resid" if glu else "mm_resid",
    )(a, *([w] * n_split), x, ap, as_)


def _s5_scan(bu_ref, st_ref, are_ref, aim_ref, nb, tc):
    lc = 512
    n_tiles = nb // 8

    def tile(r):
        r0 = r * 8
        for c in range(SSM_ST // lc):
            cre = pl.ds(c * lc, lc)
            cim = pl.ds(SSM_ST + c * lc, lc)
            ar = are_ref[:, cre]
            ai = aim_ref[:, cre]
            if isinstance(r0, int):
                srow = pl.ds(r0, 8)
            else:
                srow = pl.ds(pl.multiple_of(r0, 8), 8)
            hr = st_ref[srow, cre]
            hi = st_ref[srow, cim]

            def step(t, carry):
                hr, hi = carry
                row = pl.ds(pl.multiple_of(t * nb + r0, 8), 8)
                nr = ar * hr - ai * hi + bu_ref[row, cre]
                ni = ar * hi + ai * hr + bu_ref[row, cim]
                bu_ref[row, cre] = nr
                bu_ref[row, cim] = ni
                return nr, ni

            hr, hi = lax.fori_loop(0, tc, step, (hr, hi), unroll=8)
            st_ref[srow, cre] = hr
            st_ref[srow, cim] = hi

    if n_tiles == 1:
        tile(0)
    else:
        def body(r, c):
            tile(r)
            return c
        lax.fori_loop(0, n_tiles, body, 0)


def _s5_core(u, bu_ref, st_ref, h0re_ref, h0im_ref, are_ref, aim_ref, bw_ref, cw_ref, dsk_ref,
             sre_ref, sim_ref, nb, tc):
    t = pl.program_id(1)

    @pl.when(t == 0)
    def _():
        st_ref[:, 0:SSM_ST] = h0re_ref[...]
        st_ref[:, SSM_ST:2 * SSM_ST] = h0im_ref[...]

    bu_ref[...] = _dotb(u, bw_ref[...])
    _s5_scan(bu_ref, st_ref, are_ref, aim_ref, nb, tc)
    y = _dot(bu_ref[...].astype(BF16), cw_ref[...].astype(BF16)) + dsk_ref[...] * u

    @pl.when(t == pl.num_programs(1) - 1)
    def _():
        sre_ref[...] = st_ref[:, 0:SSM_ST]
        sim_ref[...] = st_ref[:, SSM_ST:2 * SSM_ST]

    return _gelu_tanh(y)


def _s5_prompt_kernel(*refs, nbatch, tc):
    u_refs = refs[:nbatch]
    (h0re_ref, h0im_ref, are_ref, aim_ref, bw_ref, cw_ref, dsk_ref,
     p_ref, sre_ref, sim_ref, r_ref, bu_ref, st_ref, y_ref) = refs[nbatch:]
    nslab = r_ref.shape[0]
    r_ref[...] = jnp.zeros_like(r_ref)
    for b in range(nbatch):
        ub = u_refs[b][...]
        for s in range(nslab):
            r_ref[s, pl.ds(b, tc, stride=8), :] = ub[:, s * 128:(s + 1) * 128]
    u = jnp.concatenate([r_ref[s] for s in range(nslab)], axis=1)
    y = _s5_core(u, bu_ref, st_ref, h0re_ref, h0im_ref, are_ref, aim_ref, bw_ref, cw_ref,
                 dsk_ref, sre_ref, sim_ref, 8, tc)
    for s in range(nslab):
        y_ref[s] = y[:, s * 128:(s + 1) * 128]
    for b in range(nbatch):
        p_ref[b] = jnp.concatenate([y_ref[s, pl.ds(b, tc, stride=8), :] for s in range(nslab)],
                                   axis=1).astype(p_ref.dtype)


def _s5_sample_kernel(u_ref, h0re_ref, h0im_ref, are_ref, aim_ref, bw_ref, cw_ref, dsk_ref,
                      p_ref, sre_ref, sim_ref, bu_ref, st_ref, *, nb, tc):
    p_ref[...] = _s5_core(u_ref[...], bu_ref, st_ref, h0re_ref, h0im_ref, are_ref, aim_ref, bw_ref, cw_ref,
                          dsk_ref, sre_ref, sim_ref, nb, tc).astype(p_ref.dtype)


def _s5_params(a_re, a_im, log_dt, b_re, b_im, c_re, c_im):
    g, p = a_re.shape
    dt = jnp.exp(log_dt)[:, None]
    mag = jnp.exp(dt * a_re)
    ang = dt * a_im
    ab_re = mag * jnp.cos(ang)
    ab_im = mag * jnp.sin(ang)
    n_re = ab_re - 1.0
    n_im = ab_im
    den = a_re * a_re + a_im * a_im
    f_re = ((n_re * a_re + n_im * a_im) / den)[..., None]
    f_im = ((n_im * a_re - n_re * a_im) / den)[..., None]
    bb_re = f_re * b_re - f_im * b_im
    bb_im = f_re * b_im + f_im * b_re
    ngb = g // SSM_GB
    eye = jnp.eye(SSM_GB, dtype=F32)

    def bdiag_b(bb):
        t = bb.reshape(ngb, SSM_GB, p, SSM_GROUP).transpose(0, 1, 3, 2)
        t = t[:, :, :, None, :] * eye[None, :, None, :, None]
        return t.reshape(ngb, SSM_CH, SSM_ST)

    def bdiag_c(cc):
        t = cc.reshape(ngb, SSM_GB, SSM_GROUP, p).transpose(0, 1, 3, 2)
        t = t[:, :, :, None, :] * eye[None, :, None, :, None]
        return t.reshape(ngb, SSM_ST, SSM_CH)

    bw = jnp.concatenate([bdiag_b(bb_re), bdiag_b(bb_im)], axis=2)
    cw = jnp.concatenate([bdiag_c(c_re), -bdiag_c(c_im)], axis=1)
    are = jnp.broadcast_to(ab_re.reshape(1, g * p), (8, g * p))
    aim = jnp.broadcast_to(ab_im.reshape(1, g * p), (8, g * p))
    return are, aim, bw, cw


def _s5_common_specs(nb):
    return [
        pl.BlockSpec((nb, SSM_ST), lambda gb, t: (0, gb)),
        pl.BlockSpec((nb, SSM_ST), lambda gb, t: (0, gb)),
        pl.BlockSpec((8, SSM_ST), lambda gb, t: (0, gb)),
        pl.BlockSpec((8, SSM_ST), lambda gb, t: (0, gb)),
        pl.BlockSpec((None, SSM_CH, 2 * SSM_ST), lambda gb, t: (gb, 0, 0)),
        pl.BlockSpec((None, 2 * SSM_ST, SSM_CH), lambda gb, t: (gb, 0, 0)),
        pl.BlockSpec((1, SSM_CH), lambda gb, t: (0, gb)),
    ]


def _s5_prompt(h, nbatch, lp, params, dsk):
    are, aim, bw, cw = params
    d = h.shape[1]
    ngb = d // SSM_CH
    tc = 128
    nt = lp // tc
    gp = are.shape[1]
    zeros = jnp.zeros((8, gp), F32)
    u_specs = [pl.BlockSpec((tc, SSM_CH), functools.partial(lambda gb, t, b: (b * nt + t, gb), b=b))
               for b in range(nbatch)]
    rows = tc * 8
    return pl.pallas_call(
        functools.partial(_s5_prompt_kernel, nbatch=nbatch, tc=tc),
        out_shape=(jax.ShapeDtypeStruct((nbatch, lp, d), BF16),
                   jax.ShapeDtypeStruct((8, gp), F32), jax.ShapeDtypeStruct((8, gp), F32)),
        grid=(ngb, nt),
        in_specs=u_specs + _s5_common_specs(8),
        out_specs=(pl.BlockSpec((nbatch, tc, SSM_CH), lambda gb, t: (0, t, gb)),
                   pl.BlockSpec((8, SSM_ST), lambda gb, t: (0, gb)),
                   pl.BlockSpec((8, SSM_ST), lambda gb, t: (0, gb))),
        scratch_shapes=[pltpu.VMEM((SSM_CH // 128, rows, 128), F32), pltpu.VMEM((rows, 2 * SSM_ST), F32),
                        pltpu.VMEM((8, 2 * SSM_ST), F32), pltpu.VMEM((SSM_CH // 128, rows, 128), F32)],
        compiler_params=_cparams(("parallel", "arbitrary")),
        name="s5_prompt",
    )(*([h] * nbatch), zeros, zeros, are, aim, bw, cw, dsk)


def _s5_sample(h, mp, bs, ls, h0re, h0im, params, dsk):
    are, aim, bw, cw = params
    d = h.shape[1]
    ngb = d // SSM_CH
    rows = ls * bs
    gp = are.shape[1]
    blk0 = mp // rows
    return pl.pallas_call(
        functools.partial(_s5_sample_kernel, nb=bs, tc=ls),
        out_shape=(jax.ShapeDtypeStruct((rows, d), BF16),
                   jax.ShapeDtypeStruct((bs, gp), F32), jax.ShapeDtypeStruct((bs, gp), F32)),
        grid=(ngb, 1),
        in_specs=[pl.BlockSpec((rows, SSM_CH), lambda gb, t: (blk0, gb))] + _s5_common_specs(bs),
        out_specs=(pl.BlockSpec((rows, SSM_CH), lambda gb, t: (0, gb)),
                   pl.BlockSpec((bs, SSM_ST), lambda gb, t: (0, gb)),
                   pl.BlockSpec((bs, SSM_ST), lambda gb, t: (0, gb))),
        scratch_shapes=[pltpu.VMEM((rows, 2 * SSM_ST), F32), pltpu.VMEM((bs, 2 * SSM_ST), F32)],
        compiler_params=_cparams(("parallel", "arbitrary")),
        name="s5_sample",
    )(h, h0re, h0im, are, aim, bw, cw, dsk)


def _ln_silu(y, g, b):
    mu = jnp.mean(y, axis=-1, keepdims=True)
    yc = y - mu
    z = yc * lax.rsqrt(jnp.mean(yc * yc, axis=-1, keepdims=True) + EPS) * g + b
    return _silu(z)


def _conv_prompt_kernel(cur_ref, halo_ref, buf_ref, w_ref, *rest, width, mode):
    if mode == "cf":
        bdw_ref, lng_ref, lnb_ref, o_ref, s_ref, acc_ref = rest
    else:
        bg_ref, o_ref, s_ref, acc_ref = rest
    t = pl.program_id(1)
    hb = halo_ref.shape[0]
    tc, d = cur_ref.shape
    first = t == 0
    s_ref[0:hb, :] = jnp.where(first, buf_ref[0], halo_ref[...])
    s_ref[hb:hb + tc, :] = cur_ref[...]
    off = hb - (width - 1)
    lch = 256
    for c in range(d // lch):
        cols = pl.ds(c * lch, lch)
        acc = w_ref[0:1, cols] * s_ref[off:off + tc, cols]
        for k in range(1, width):
            acc = acc + w_ref[k:k + 1, cols] * s_ref[off + k:off + k + tc, cols]
        acc_ref[:, cols] = acc
    if mode == "cf":
        o_ref[...] = _ln_silu(acc_ref[...] + bdw_ref[...], lng_ref[...], lnb_ref[...]).astype(o_ref.dtype)
    else:
        o_ref[...] = (bg_ref[...] * acc_ref[...]).astype(o_ref.dtype)


def _conv_prompt(v, nbatch, lp, w, layer, mode, extra):
    d = v.shape[1]
    width = w.shape[1]
    hb = 32 if width > 9 else 8
    tc = 256
    nt = lp // tc
    zeros = jnp.zeros((nbatch, hb, d), F32)
    in_specs = [
        pl.BlockSpec((tc, d), lambda b, t: (b * nt + t, 0)),
        pl.BlockSpec((hb, d), lambda b, t: (jnp.maximum((b * lp + t * tc) // hb - 1, 0), 0)),
        pl.BlockSpec((1, hb, d), lambda b, t: (b, 0, 0)),
        pl.BlockSpec((None, width, d), lambda b, t: (layer, 0, 0)),
    ]
    if mode == "cf":
        in_specs += [pl.BlockSpec((1, d), lambda b, t: (layer, 0))] * 3
        args = extra
    else:
        in_specs += [pl.BlockSpec((tc, d), lambda b, t: (b * nt + t, 0))]
        args = extra
    return pl.pallas_call(
        functools.partial(_conv_prompt_kernel, width=width, mode=mode),
        out_shape=jax.ShapeDtypeStruct((nbatch * lp, d), BF16),
        grid=(nbatch, nt),
        in_specs=in_specs,
        out_specs=pl.BlockSpec((tc, d), lambda b, t: (b * nt + t, 0)),
        scratch_shapes=[pltpu.VMEM((hb + tc, d), F32), pltpu.VMEM((tc, d), F32)],
        compiler_params=_cparams(("parallel", "arbitrary")),
        name="conv_prompt_" + mode,
    )(v, v, zeros, w, *args)


def _conv_sample_kernel(v_ref, cache_ref, w_ref, *rest, width, mode):
    if mode == "cf":
        bdw_ref, lng_ref, lnb_ref, o_ref = rest
    else:
        bg_ref, o_ref = rest
    ls = v_ref.shape[0]
    hist = width - 1

    def full(j):
        return cache_ref[j] if j < hist else v_ref[j - hist]

    for l in range(ls):
        acc = w_ref[0:1, :] * full(l)
        for k in range(1, width):
            acc = acc + w_ref[k:k + 1, :] * full(l + k)
        if mode == "cf":
            o_ref[l] = _ln_silu(acc + bdw_ref[...], lng_ref[...], lnb_ref[...]).astype(o_ref.dtype)
        else:
            o_ref[l] = (bg_ref[l] * acc).astype(o_ref.dtype)


def _conv_sample(v_tm, cache_tm, w, layer, mode, extra):
    ls, bs, d = v_tm.shape
    width = w.shape[1]
    bc = 16
    in_specs = [
        pl.BlockSpec((ls, bc, d), lambda i: (0, i, 0)),
        pl.BlockSpec((width - 1, bc, d), lambda i: (0, i, 0)),
        pl.BlockSpec((None, width, d), lambda i: (layer, 0, 0)),
    ]
    if mode == "cf":
        in_specs += [pl.BlockSpec((1, d), lambda i: (layer, 0))] * 3
    else:
        in_specs += [pl.BlockSpec((ls, bc, d), lambda i: (0, i, 0))]
    return pl.pallas_call(
        functools.partial(_conv_sample_kernel, width=width, mode=mode),
        out_shape=jax.ShapeDtypeStruct((ls, bs, d), BF16),
        grid=(bs // bc,),
        in_specs=in_specs,
        out_specs=pl.BlockSpec((ls, bc, d), lambda i: (0, i, 0)),
        compiler_params=_cparams(("parallel",)),
        name="conv_sample_" + mode,
    )(v_tm, cache_tm, w, *extra)


def _moe_dispatch(route, n_blocks):
    m = route.shape[0]
    n_asg = 2 * m
    flat_e = route[:, 0:2].astype(jnp.int32).reshape(n_asg)
    order = jnp.argsort(flat_e).astype(jnp.int32)
    se = flat_e[order]
    counts = jnp.zeros((N_EXPERTS,), jnp.int32).at[flat_e].add(1)
    starts = jnp.cumsum(counts) - counts
    padded = (counts + MOE_BM - 1) // MOE_BM * MOE_BM
    pends = jnp.cumsum(padded)
    pstarts = pends - padded
    dest = pstarts[se] + (jnp.arange(n_asg, dtype=jnp.int32) - starts[se])
    n_rows = n_blocks * MOE_BM
    row_tok = jnp.zeros((n_rows,), jnp.int32).at[dest].set(order // 2)
    pos = jnp.zeros((n_asg,), jnp.int32).at[order].set(dest)
    n_used = (pends[-1] // MOE_BM).astype(jnp.int32)
    blk = jnp.arange(n_blocks, dtype=jnp.int32)
    blk_e = jnp.minimum(jnp.searchsorted(pends, blk * MOE_BM, side="right"), N_EXPERTS - 1).astype(jnp.int32)
    blk_e = jnp.where(blk < n_used, blk_e, se[-1])
    return blk_e, row_tok, n_used.reshape(1), pos


def _moe_expert_kernel(be_ref, rt_ref, nu_ref, h_ref, w13_ref, w2_ref, y_ref, xbuf, sem):
    i = pl.program_id(0)
    n = pl.num_programs(0)
    slot = i % 2

    def gather(blk, s):
        def row(r, c):
            tok = rt_ref[blk * MOE_BM + r]
            pltpu.make_async_copy(h_ref.at[pl.ds(tok, 1)], xbuf.at[s, pl.ds(r, 1)], sem.at[s]).start()
            return c
        lax.fori_loop(0, MOE_BM, row, 0, unroll=8)

    @pl.when(i == 0)
    def _():
        gather(0, 0)

    @pl.when(i + 1 < n)
    def _():
        gather(i + 1, 1 - slot)

    pltpu.make_async_copy(h_ref.at[pl.ds(0, MOE_BM)], xbuf.at[slot], sem.at[slot]).wait()

    @pl.when(i < nu_ref[0])
    def _():
        x = xbuf[slot].astype(BF16)
        ab = _dot(x, w13_ref[...].astype(BF16))
        a = ab[:, 0:D_EXPERT]
        b = ab[:, D_EXPERT:2 * D_EXPERT]
        hm = (_silu(a) * b).astype(BF16)
        y_ref[...] = _dot(hm, w2_ref[...].astype(BF16))

    @pl.when(i >= nu_ref[0])
    def _():
        y_ref[...] = jnp.zeros_like(y_ref)


def _moe_experts(h, blk_e, row_tok, n_used, w13, w2, layer):
    m, d = h.shape
    n_blocks = blk_e.shape[0]
    de2 = w13.shape[3]
    grid_spec = pltpu.PrefetchScalarGridSpec(
        num_scalar_prefetch=3,
        grid=(n_blocks,),
        in_specs=[
            pl.BlockSpec(memory_space=pl.ANY),
            pl.BlockSpec((None, None, d, de2), lambda i, be, rt, nu: (layer, be[i], 0, 0)),
            pl.BlockSpec((None, None, de2 // 2, d), lambda i, be, rt, nu: (layer, be[i], 0, 0)),
        ],
        out_specs=pl.BlockSpec((MOE_BM, d), lambda i, be, rt, nu: (i, 0)),
        scratch_shapes=[pltpu.VMEM((2, MOE_BM, d), F32), pltpu.SemaphoreType.DMA((2,))],
    )
    return pl.pallas_call(
        _moe_expert_kernel,
        out_shape=jax.ShapeDtypeStruct((n_blocks * MOE_BM, d), F32),
        grid_spec=grid_spec,
        compiler_params=_cparams(("arbitrary",)),
        name="moe_experts",
    )(blk_e, row_tok, n_used, h, w13, w2)


def _moe_combine_kernel(pos_ref, y_ref, x_ref, r_ref, gp_ref, gs_ref, o_ref, ybuf, sem, *, npb, bpb):
    i = pl.program_id(0)
    n = pl.num_programs(0)
    slot = i % 2

    def gather(blk, s):
        def row(r, c):
            base = (blk * TMC + r) * 2
            for k in range(2):
                p = pos_ref[base + k]
                pltpu.make_async_copy(y_ref.at[pl.ds(p, 1)], ybuf.at[s, k, pl.ds(r, 1)], sem.at[s]).start()
            return c
        lax.fori_loop(0, TMC, row, 0, unroll=4)

    @pl.when(i == 0)
    def _():
        gather(0, 0)

    @pl.when(i + 1 < n)
    def _():
        gather(i + 1, 1 - slot)

    for k in range(2):
        pltpu.make_async_copy(y_ref.at[pl.ds(0, TMC)], ybuf.at[slot, k], sem.at[slot]).wait()

    r = r_ref[...]
    f = r[:, 2:3] * ybuf[slot, 0] + r[:, 3:4] * ybuf[slot, 1]
    gate = _seq_vec(gp_ref, gs_ref, i, npb, bpb, TMC)
    o_ref[...] = x_ref[...] + gate * f


def _moe_combine(y, pos, x, route, ap, as_, gate_idx, mp, lp):
    m, d = x.shape
    bs = as_.shape[0]
    grid_spec = pltpu.PrefetchScalarGridSpec(
        num_scalar_prefetch=1,
        grid=(m // TMC,),
        in_specs=[
            pl.BlockSpec(memory_space=pl.ANY),
            pl.BlockSpec((TMC, d), lambda i, pos: (i, 0)),
            pl.BlockSpec((TMC, ROUTE_LANES), lambda i, pos: (i, 0)),
            pl.BlockSpec((8, d), lambda i, pos: (0, gate_idx)),
            pl.BlockSpec((bs, d), lambda i, pos: (0, gate_idx)),
        ],
        out_specs=pl.BlockSpec((TMC, d), lambda i, pos: (i, 0)),
        scratch_shapes=[pltpu.VMEM((2, 2, TMC, d), F32), pltpu.SemaphoreType.DMA((2,))],
    )
    return pl.pallas_call(
        functools.partial(_moe_combine_kernel, npb=mp // TMC, bpb=lp // TMC),
        out_shape=jax.ShapeDtypeStruct((m, d), F32),
        grid_spec=grid_spec,
        compiler_params=_cparams(("arbitrary",)),
        name="moe_combine",
    )(pos, y, x, route, ap, as_)


def kernel(x_prompt, x_sample, c_prompt, c_sample, state_l0_ssm_re, state_l0_ssm_im, cache_l1_conformer_conv, cache_l2_short_conv, state_l3_ssm_re, state_l3_ssm_im, norm1_g, norm2_g, w_ada, b_ada, final_norm_g, ssm_a_re, ssm_a_im, ssm_log_dt, ssm_b_re, ssm_b_im, ssm_c_re, ssm_c_im, ssm_d, ssm_w_glu, cf_w_pw1, cf_w_dw, cf_b_dw, cf_ln_g, cf_ln_b, cf_w_pw2, sc_w_in, sc_w_conv, sc_w_out, moe_w_group, moe_b_group, moe_w_expert, moe_b_expert, moe_w13, moe_w2):
    bp, lp, d = x_prompt.shape
    bs, ls, _ = x_sample.shape
    depth = w_ada.shape[0]
    mp, ms = bp * lp, bs * ls
    m = mp + ms
    g_ssm, p_ssm = ssm_a_re.shape[1:]
    assert d == D_MODEL and lp % TM == 0 and ms % TM == 0 and TM % bs == 0 and bp <= 8 and bs % 16 == 0
    assert TMC % bs == 0 and mp % ms == 0

    x = jnp.concatenate([x_prompt.reshape(mp, d), x_sample.transpose(1, 0, 2).reshape(ms, d)], axis=0)
    c_all = jnp.concatenate([c_prompt, jnp.zeros((8 - bp, d), F32), c_sample], axis=0)
    ada = _ada_all(c_all, w_ada, b_ada)

    sample_init = [(state_l0_ssm_re, state_l0_ssm_im), (cache_l1_conformer_conv,), (cache_l2_short_conv,),
                   (state_l3_ssm_re, state_l3_ssm_im)]
    n_blocks = (2 * m + N_EXPERTS * (MOE_BM - 1) + MOE_BM - 1) // MOE_BM
    p_states, s_states = [], []

    for i in range(depth):
        ap, as_ = ada[i, 0:8], ada[i, 8:]
        kind, j = i % 3, i // 3
        if kind == 0:
            h = _modulate(x, norm1_g[i:i + 1], ap, as_, 0, 1, mp, lp, F32)
            params = _s5_params(ssm_a_re[j], ssm_a_im[j], ssm_log_dt[j], ssm_b_re[j], ssm_b_im[j],
                                ssm_c_re[j], ssm_c_im[j])
            dsk = ssm_d[j:j + 1]
            pp, pre, pim = _s5_prompt(h, bp, lp, params, dsk)
            h0re, h0im = sample_init[i]
            psm, sre, sim = _s5_sample(h, mp, bs, ls, h0re.reshape(bs, g_ssm * p_ssm),
                                       h0im.reshape(bs, g_ssm * p_ssm), params, dsk)
            p_states.append((pre[:bp].reshape(bp, g_ssm, p_ssm), pim[:bp].reshape(bp, g_ssm, p_ssm)))
            s_states.append((sre.reshape(bs, g_ssm, p_ssm), sim.reshape(bs, g_ssm, p_ssm)))
            pre_out = jnp.concatenate([pp.reshape(mp, d), psm], axis=0)
            w_out = ssm_w_glu
        elif kind == 1:
            h = _modulate(x, norm1_g[i:i + 1], ap, as_, 0, 1, mp, lp, BF16)
            gl = _mm_split(h, cf_w_pw1, j, 2, _mm_glu_kernel, 1, 512, "mm_glu")
            extra = (cf_b_dw, cf_ln_g, cf_ln_b)
            pp = _conv_prompt(gl, bp, lp, cf_w_dw, j, "cf", extra)
            (cache,) = sample_init[i]
            g_tm = gl[mp:].reshape(ls, bs, d)
            psm = _conv_sample(g_tm, cache.transpose(1, 0, 2), cf_w_dw, j, "cf", extra)
            hist = CF_WIDTH - 1
            p_states.append((gl[:mp].reshape(bp, lp, d)[:, lp - hist:],))
            s_states.append((jnp.concatenate([cache, g_tm.transpose(1, 0, 2)], axis=1)[:, -hist:],))
            pre_out = jnp.concatenate([pp, psm.reshape(ms, d)], axis=0)
            w_out = cf_w_pw2
        else:
            h = _modulate(x, norm1_g[i:i + 1], ap, as_, 0, 1, mp, lp, BF16)
            bg, v = _mm_split(h, sc_w_in, j, 3, _mm_bcv_kernel, 2, 512, "mm_bcv")
            pp = _conv_prompt(v, bp, lp, sc_w_conv, j, "sc", (bg,))
            (cache,) = sample_init[i]
            v_tm = v[mp:].reshape(ls, bs, d)
            psm = _conv_sample(v_tm, cache.transpose(1, 0, 2), sc_w_conv, j, "sc", (bg[mp:].reshape(ls, bs, d),))
            hist = SC_WIDTH - 1
            p_states.append((v[:mp].reshape(bp, lp, d)[:, lp - hist:],))
            s_states.append((jnp.concatenate([cache, v_tm.transpose(1, 0, 2)], axis=1)[:, -hist:],))
            pre_out = jnp.concatenate([pp, psm.reshape(ms, d)], axis=0)
            w_out = sc_w_out

        x = _mm_resid(pre_out, w_out, j, x, ap, as_, 2, mp, lp, glu=(kind == 0))

        wr = jnp.concatenate([moe_w_group[i], moe_w_expert[i],
                              jnp.zeros((d, ROUTE_LANES - N_GROUPS - N_EXPERTS), F32)], axis=1)
        br = jnp.concatenate([moe_b_group[i], moe_b_expert[i],
                              jnp.zeros((ROUTE_LANES - N_GROUPS - N_EXPERTS,), F32)]).reshape(1, ROUTE_LANES)
        h2, route = _modulate(x, norm2_g[i:i + 1], ap, as_, 3, 4, mp, lp, F32, router=(wr, br))
        blk_e, row_tok, n_used, pos = _moe_dispatch(route, n_blocks)
        y = _moe_experts(h2, blk_e, row_tok, n_used, moe_w13, moe_w2, i)
        x = _moe_combine(y, pos, x, route, ap, as_, 5, mp, lp)

    out = _final_norm(x, final_norm_g.reshape(1, d))
    y_prompt = out[:mp].reshape(bp, lp, d)
    y_sample = out[mp:].reshape(ls, bs, d).transpose(1, 0, 2)
    ps, ss = p_states, s_states
    return (y_prompt, y_sample,
            ps[0][0], ps[0][1], ps[1][0], ps[2][0], ps[3][0], ps[3][1],
            ss[0][0], ss[0][1], ss[1][0], ss[2][0], ss[3][0], ss[3][1])
```

```python
import functools

import jax
import jax.numpy as jnp
from jax import lax
from jax.experimental import pallas as pl
from jax.experimental.pallas import tpu as pltpu

F32 = jnp.float32
BF16 = jnp.bfloat16
I32 = jnp.int32

D_MODEL = 2048
SSM_GROUP = 16
SSM_STATE = 64
SSM_GB = 16
SSM_CH = SSM_GB * SSM_GROUP
SSM_ST = SSM_GB * SSM_STATE
CF_WIDTH = 31
SC_WIDTH = 3
N_GROUPS = 8
EPG = 8
N_EXPERTS = N_GROUPS * EPG
D_EXPERT = D_MODEL // 4
EPS = 1e-6

TM = 512
MOE_BM = 128
TMC = 256
PLAN_TB = 512
PLAN_ROWS = 256
ROUTE_LANES = 128
VMEM_LIMIT = 56 * 1024 * 1024


def _cparams(sem):
    return pltpu.CompilerParams(dimension_semantics=sem, vmem_limit_bytes=VMEM_LIMIT)


def _dot(a, b):
    return jnp.dot(a, b, preferred_element_type=F32)


def _dotb(a, b):
    return _dot(a.astype(BF16), b.astype(BF16))


def _sigmoid(x):
    return 1.0 / (1.0 + jnp.exp(-x))


def _silu(x):
    return x * _sigmoid(x)


def _gelu_tanh(x):
    return 0.5 * x * (1.0 + jnp.tanh(0.7978845608028654 * (x + 0.044715 * (x * x * x))))


def _seq_vec(vp_ref, vs_ref, blk, n_prompt_blocks, blocks_per_batch, rows):
    b = jnp.minimum(blk // blocks_per_batch, vp_ref.shape[0] - 1)
    vp = vp_ref[pl.ds(b, 1), :]
    vs = vs_ref[...]
    reps = rows // vs.shape[0]
    if reps > 1:
        vs = jnp.concatenate([vs] * reps, axis=0)
    return jnp.where(blk < n_prompt_blocks, vp, vs)


def _ada_kernel(c_ref, w_ref, b_ref, o_ref):
    c = c_ref[...]
    o_ref[0] = _dotb(_silu(c), w_ref[0]) + b_ref[0]


def _ada_all(c_all, w_ada, b_ada):
    depth, d, n = w_ada.shape
    rows = c_all.shape[0]
    tn = 1024
    return pl.pallas_call(
        _ada_kernel,
        out_shape=jax.ShapeDtypeStruct((depth, rows, n), F32),
        grid=(depth, n // tn),
        in_specs=[
            pl.BlockSpec((rows, d), lambda l, j: (0, 0)),
            pl.BlockSpec((1, d, tn), lambda l, j: (l, 0, j)),
            pl.BlockSpec((1, 1, tn), lambda l, j: (l, 0, j)),
        ],
        out_specs=pl.BlockSpec((1, rows, tn), lambda l, j: (l, 0, j)),
        compiler_params=_cparams(("parallel", "parallel")),
        name="ada_proj",
    )(c_all, w_ada, b_ada.reshape(depth, 1, n))


def _rms(x, g):
    return x * lax.rsqrt(jnp.mean(x * x, axis=-1, keepdims=True) + EPS) * g


def _modulate_kernel(x_ref, g_ref, shp_ref, scp_ref, shs_ref, scs_ref, o_ref, *, npb, bpb):
    i = pl.program_id(0)
    rows = x_ref.shape[0]
    y = _rms(x_ref[...], g_ref[...])
    sc = _seq_vec(scp_ref, scs_ref, i, npb, bpb, rows)
    sh = _seq_vec(shp_ref, shs_ref, i, npb, bpb, rows)
    o_ref[...] = (y * (1.0 + sc) + sh).astype(o_ref.dtype)


def _route_from_logits(lg):
    rows = lg.shape[0]
    lane = lax.broadcasted_iota(I32, (rows, ROUTE_LANES), 1)
    lane_f = lane.astype(F32)
    neg = jnp.float32(-jnp.inf)
    big = jnp.float32(ROUTE_LANES)
    is_g = lane < N_GROUPS
    gl = jnp.where(is_g, lg, neg)
    gmax = jnp.max(gl, axis=1, keepdims=True)
    grp = jnp.min(jnp.where(gl == gmax, lane_f, big), axis=1, keepdims=True)
    p_grp = 1.0 / jnp.sum(jnp.where(is_g, jnp.exp(gl - gmax), 0.0), axis=1, keepdims=True)
    e_grp = ((lane - N_GROUPS) >> 3).astype(F32)
    valid = (lane >= N_GROUPS) & (lane < N_GROUPS + N_EXPERTS) & (e_grp == grp)
    el = jnp.where(valid, lg, neg)
    v1 = jnp.max(el, axis=1, keepdims=True)
    i1 = jnp.min(jnp.where(el == v1, lane_f, big), axis=1, keepdims=True)
    el2 = jnp.where(lane_f == i1, neg, el)
    v2 = jnp.max(el2, axis=1, keepdims=True)
    i2 = jnp.min(jnp.where(el2 == v2, lane_f, big), axis=1, keepdims=True)
    t = jnp.exp(v2 - v1)
    den = 1.0 + t
    w1 = (1.0 / den) * p_grp
    w2 = (t / den) * p_grp
    out = jnp.where(lane == 0, i1 - N_GROUPS, 0.0)
    out = jnp.where(lane == 1, i2 - N_GROUPS, out)
    out = jnp.where(lane == 2, w1, out)
    out = jnp.where(lane == 3, w2, out)
    return out


def _modulate_route_kernel(x_ref, g_ref, shp_ref, scp_ref, shs_ref, scs_ref, wr_ref, br_ref,
                           o_ref, r_ref, *, npb, bpb):
    i = pl.program_id(0)
    rows = x_ref.shape[0]
    y = _rms(x_ref[...], g_ref[...])
    sc = _seq_vec(scp_ref, scs_ref, i, npb, bpb, rows)
    sh = _seq_vec(shp_ref, shs_ref, i, npb, bpb, rows)
    h = y * (1.0 + sc) + sh
    o_ref[...] = h
    lg = _dotb(h, wr_ref[...]) + br_ref[...]
    r_ref[...] = _route_from_logits(lg)


def _modulate(x, g, ap, as_, sh_idx, sc_idx, mp, lp, out_dtype, router=None):
    m, d = x.shape
    bs = as_.shape[0]
    npb, bpb = mp // TM, lp // TM
    in_specs = [
        pl.BlockSpec((TM, d), lambda i: (i, 0)),
        pl.BlockSpec((1, d), lambda i: (0, 0)),
        pl.BlockSpec((8, d), lambda i: (0, sh_idx)),
        pl.BlockSpec((8, d), lambda i: (0, sc_idx)),
        pl.BlockSpec((bs, d), lambda i: (0, sh_idx)),
        pl.BlockSpec((bs, d), lambda i: (0, sc_idx)),
    ]
    if router is None:
        return pl.pallas_call(
            functools.partial(_modulate_kernel, npb=npb, bpb=bpb),
            out_shape=jax.ShapeDtypeStruct((m, d), out_dtype),
            grid=(m // TM,),
            in_specs=in_specs,
            out_specs=pl.BlockSpec((TM, d), lambda i: (i, 0)),
            compiler_params=_cparams(("parallel",)),
            name="modulate",
        )(x, g, ap, ap, as_, as_)
    wr, br = router
    return pl.pallas_call(
        functools.partial(_modulate_route_kernel, npb=npb, bpb=bpb),
        out_shape=(jax.ShapeDtypeStruct((m, d), F32), jax.ShapeDtypeStruct((m, ROUTE_LANES), F32)),
        grid=(m // TM,),
        in_specs=in_specs + [
            pl.BlockSpec((d, ROUTE_LANES), lambda i: (0, 0)),
            pl.BlockSpec((1, ROUTE_LANES), lambda i: (0, 0)),
        ],
        out_specs=(pl.BlockSpec((TM, d), lambda i: (i, 0)), pl.BlockSpec((TM, ROUTE_LANES), lambda i: (i, 0))),
        compiler_params=_cparams(("parallel",)),
        name="modulate_route",
    )(x, g, ap, ap, as_, as_, wr, br)


def _final_norm_kernel(x_ref, g_ref, op_ref, os_ref, *, npb):
    i = pl.program_id(0)
    y = _rms(x_ref[...], g_ref[...])

    @pl.when(i < npb)
    def _():
        op_ref[...] = y

    @pl.when(i >= npb)
    def _():
        os_ref[...] = y


def _final_norm(x, g, mp):
    m, d = x.shape
    npb = mp // TM
    return pl.pallas_call(
        functools.partial(_final_norm_kernel, npb=npb),
        out_shape=(jax.ShapeDtypeStruct((mp, d), F32), jax.ShapeDtypeStruct((m - mp, d), F32)),
        grid=(m // TM,),
        in_specs=[pl.BlockSpec((TM, d), lambda i: (i, 0)), pl.BlockSpec((1, d), lambda i: (0, 0))],
        out_specs=(pl.BlockSpec((TM, d), lambda i: (jnp.minimum(i, npb - 1), 0)),
                   pl.BlockSpec((TM, d), lambda i: (jnp.maximum(i - npb, 0), 0))),
        compiler_params=_cparams(("arbitrary",)),
        name="final_norm",
    )(x, g)


def _a_specs(a_parts, npb):
    k = a_parts[0].shape[1]
    if len(a_parts) == 1:
        return [pl.BlockSpec((TM, k), lambda j, i: (i, 0))]
    return [pl.BlockSpec((TM, k), lambda j, i: (jnp.minimum(i, npb - 1), 0)),
            pl.BlockSpec((TM, k), lambda j, i: (jnp.maximum(i - npb, 0), 0))]


def _mm_products(a_refs, w_refs, wbf_ref, npb):
    i = pl.program_id(1)

    @pl.when(i == 0)
    def _():
        for s, w_ref in enumerate(w_refs):
            wbf_ref[s] = w_ref[...].astype(BF16)

    if len(a_refs) == 1:
        a = a_refs[0][...]
    else:
        a = jnp.where(i < npb, a_refs[0][...], a_refs[1][...])
    a = a.astype(BF16)
    return [_dot(a, wbf_ref[s]) for s in range(len(w_refs))]


def _mm_glu_kernel(*refs, n_a, npb):
    a_refs, (wa_ref, wb_ref, o_ref, wbf_ref) = refs[:n_a], refs[n_a:]
    za, zb = _mm_products(a_refs, (wa_ref, wb_ref), wbf_ref, npb)
    o_ref[...] = za * _sigmoid(zb)


def _mm_bcv_kernel(*refs, n_a, npb):
    a_refs, (w0_ref, w1_ref, w2_ref, bg_ref, v_ref, wbf_ref) = refs[:n_a], refs[n_a:]
    bg, cg, hv = _mm_products(a_refs, (w0_ref, w1_ref, w2_ref), wbf_ref, npb)
    bg_ref[...] = bg
    v_ref[...] = cg * hv


def _mm_resid_kernel(*refs, n_a, npb, bpb, glu):
    a_refs, rest = refs[:n_a], refs[n_a:]
    n_w = 2 if glu else 1
    w_refs = rest[:n_w]
    x_ref, gp_ref, gs_ref, o_ref, wbf_ref = rest[n_w:]
    z = _mm_products(a_refs, w_refs, wbf_ref, npb)
    z = z[0] * _sigmoid(z[1]) if glu else z[0]
    gate = _seq_vec(gp_ref, gs_ref, pl.program_id(1), npb, bpb, x_ref.shape[0])
    o_ref[...] = x_ref[...] + gate * z


def _w_spec(k, tn, layer, col0):
    return pl.BlockSpec((None, k, tn), lambda j, i: (layer, 0, col0 + j))


def _mm_split(a_parts, w, layer, n_split, kernel_fn, n_out, tn, mp, name):
    m = sum(a.shape[0] for a in a_parts)
    k = a_parts[0].shape[1]
    n = w.shape[2] // n_split
    nj = n // tn
    out_shape = [jax.ShapeDtypeStruct((m, n), F32) for _ in range(n_out)]
    out_specs = [pl.BlockSpec((TM, tn), lambda j, i: (i, j)) for _ in range(n_out)]
    return pl.pallas_call(
        functools.partial(kernel_fn, n_a=len(a_parts), npb=mp // TM),
        out_shape=out_shape if n_out > 1 else out_shape[0],
        grid=(nj, m // TM),
        in_specs=_a_specs(a_parts, mp // TM) + [_w_spec(k, tn, layer, s * nj) for s in range(n_split)],
        out_specs=out_specs if n_out > 1 else out_specs[0],
        scratch_shapes=[pltpu.VMEM((n_split, k, tn), BF16)],
        compiler_params=_cparams(("parallel", "arbitrary")),
        name=name,
    )(*a_parts, *([w] * n_split))


def _mm_resid(a_parts, w, layer, x, ap, as_, gate_idx, mp, lp, glu=False):
    m = x.shape[0]
    k = a_parts[0].shape[1]
    n_split = 2 if glu else 1
    n = w.shape[2] // n_split
    tn = 512 if glu else 1024
    nj = n // tn
    bs = as_.shape[0]
    return pl.pallas_call(
        functools.partial(_mm_resid_kernel, n_a=len(a_parts), npb=mp // TM, bpb=lp // TM, glu=glu),
        out_shape=jax.ShapeDtypeStruct((m, n), F32),
        grid=(nj, m // TM),
        in_specs=_a_specs(a_parts, mp // TM)
        + [_w_spec(k, tn, layer, s * nj) for s in range(n_split)]
        + [
            pl.BlockSpec((TM, tn), lambda j, i: (i, j)),
            pl.BlockSpec((8, tn), lambda j, i: (0, gate_idx * nj + j)),
            pl.BlockSpec((bs, tn), lambda j, i: (0, gate_idx * nj + j)),
        ],
        out_specs=pl.BlockSpec((TM, tn), lambda j, i: (i, j)),
        scratch_shapes=[pltpu.VMEM((n_split, k, tn), BF16)],
        compiler_params=_cparams(("parallel", "arbitrary")),
        name="mm_glu_resid" if glu else "mm_resid",
    )(*a_parts, *([w] * n_split), x, ap, as_)


S5_LC = 512


def _s5_scan(bu_ref, st_ref, are_ref, aim_ref, nb, tc):
    n_tiles = nb // 8

    def tile(r, carry):
        r0 = pl.multiple_of(r * 8, 8)
        for c in range(SSM_ST // S5_LC):
            cre = pl.ds(c * S5_LC, S5_LC)
            cim = pl.ds(SSM_ST + c * S5_LC, S5_LC)
            ar = are_ref[:, cre]
            ai = aim_ref[:, cre]
            srow = pl.ds(r0, 8)
            hr = st_ref[srow, cre]
            hi = st_ref[srow, cim]

            def step(t, carry):
                hr, hi = carry
                row = pl.ds(pl.multiple_of(t * nb + r0, 8), 8)
                nr = ar * hr - ai * hi + bu_ref[row, cre]
                ni = ar * hi + ai * hr + bu_ref[row, cim]
                bu_ref[row, cre] = nr
                bu_ref[row, cim] = ni
                return nr, ni

            hr, hi = lax.fori_loop(0, tc, step, (hr, hi), unroll=8)
            st_ref[srow, cre] = hr
            st_ref[srow, cim] = hi
        return carry

    lax.fori_loop(0, n_tiles, tile, 0)


def _s5_scan_pairs(bu_ref, st_ref, are_ref, aim_ref, tc):
    first_half = lax.broadcasted_iota(I32, (8, S5_LC), 0) < 4
    for c in range(SSM_ST // S5_LC):
        cre = pl.ds(c * S5_LC, S5_LC)
        cim = pl.ds(SSM_ST + c * S5_LC, S5_LC)
        ar = are_ref[:, cre]
        ai = aim_ref[:, cre]

        def pair(j, carry):
            hr, hi = carry
            row = pl.ds(pl.multiple_of(j * 8, 8), 8)
            br = bu_ref[row, cre]
            bi = bu_ref[row, cim]
            er = ar * hr - ai * hi + br
            ei = ar * hi + ai * hr + bi
            sr = pltpu.roll(er, 4, 0)
            si = pltpu.roll(ei, 4, 0)
            orr = ar * sr - ai * si + br
            oi = ar * si + ai * sr + bi
            bu_ref[row, cre] = jnp.where(first_half, er, orr)
            bu_ref[row, cim] = jnp.where(first_half, ei, oi)
            return pltpu.roll(orr, 4, 0), pltpu.roll(oi, 4, 0)

        hr, hi = lax.fori_loop(0, tc // 2, pair, (st_ref[:, cre], st_ref[:, cim]), unroll=4)
        st_ref[:, cre] = hr
        st_ref[:, cim] = hi


def _s5_core(u, bu_ref, st_ref, h0re_ref, h0im_ref, are_ref, aim_ref, bw_ref, cw_ref, dsk_ref,
             sre_ref, sim_ref, scan):
    t = pl.program_id(1)

    @pl.when(t == 0)
    def _():
        st_ref[:, 0:SSM_ST] = h0re_ref[...]
        st_ref[:, SSM_ST:2 * SSM_ST] = h0im_ref[...]

    bu_ref[...] = _dotb(u, bw_ref[...])
    scan(bu_ref, st_ref, are_ref, aim_ref)
    y = _dot(bu_ref[...].astype(BF16), cw_ref[...].astype(BF16)) + dsk_ref[...] * u

    @pl.when(t == pl.num_programs(1) - 1)
    def _():
        sre_ref[...] = st_ref[:, 0:SSM_ST]
        sim_ref[...] = st_ref[:, SSM_ST:2 * SSM_ST]

    return _gelu_tanh(y)


def _s5_prompt_kernel(*refs, nbatch, tc):
    u_refs = refs[:nbatch]
    (h0re_ref, h0im_ref, are_ref, aim_ref, bw_ref, cw_ref, dsk_ref,
     p_ref, sre_ref, sim_ref, r_ref, bu_ref, st_ref, y_ref) = refs[nbatch:]
    nslab = r_ref.shape[0]
    for b in range(nbatch):
        ub = u_refs[b][...]
        for s in range(nslab):
            r_ref[s, pl.ds(b, tc, stride=nbatch), :] = ub[:, s * 128:(s + 1) * 128]
    u = jnp.concatenate([r_ref[s] for s in range(nslab)], axis=1)
    scan = functools.partial(_s5_scan_pairs, tc=tc)
    y = _s5_core(u, bu_ref, st_ref, h0re_ref, h0im_ref, are_ref, aim_ref, bw_ref, cw_ref,
                 dsk_ref, sre_ref, sim_ref, scan)
    for s in range(nslab):
        y_ref[s] = y[:, s * 128:(s + 1) * 128]
    for b in range(nbatch):
        p_ref[b] = jnp.concatenate([y_ref[s, pl.ds(b, tc, stride=nbatch), :] for s in range(nslab)],
                                   axis=1).astype(p_ref.dtype)


def _s5_sample_kernel(u_ref, h0re_ref, h0im_ref, are_ref, aim_ref, bw_ref, cw_ref, dsk_ref,
                      p_ref, sre_ref, sim_ref, bu_ref, st_ref, *, nb, tc):
    scan = functools.partial(_s5_scan, nb=nb, tc=tc)
    p_ref[...] = _s5_core(u_ref[...], bu_ref, st_ref, h0re_ref, h0im_ref, are_ref, aim_ref, bw_ref, cw_ref,
                          dsk_ref, sre_ref, sim_ref, scan).astype(p_ref.dtype)


def _s5_params(a_re, a_im, log_dt, b_re, b_im, c_re, c_im):
    g, p = a_re.shape
    dt = jnp.exp(log_dt)[:, None]
    mag = jnp.exp(dt * a_re)
    ang = dt * a_im
    ab_re = mag * jnp.cos(ang)
    ab_im = mag * jnp.sin(ang)
    n_re = ab_re - 1.0
    n_im = ab_im
    den = a_re * a_re + a_im * a_im
    f_re = ((n_re * a_re + n_im * a_im) / den)[..., None]
    f_im = ((n_im * a_re - n_re * a_im) / den)[..., None]
    bb_re = f_re * b_re - f_im * b_im
    bb_im = f_re * b_im + f_im * b_re
    ngb = g // SSM_GB
    eye = jnp.eye(SSM_GB, dtype=F32)

    def bdiag_b(bb):
        t = bb.reshape(ngb, SSM_GB, p, SSM_GROUP).transpose(0, 1, 3, 2)
        t = t[:, :, :, None, :] * eye[None, :, None, :, None]
        return t.reshape(ngb, SSM_CH, SSM_ST)

    def bdiag_c(cc):
        t = cc.reshape(ngb, SSM_GB, SSM_GROUP, p).transpose(0, 1, 3, 2)
        t = t[:, :, :, None, :] * eye[None, :, None, :, None]
        return t.reshape(ngb, SSM_ST, SSM_CH)

    bw = jnp.concatenate([bdiag_b(bb_re), bdiag_b(bb_im)], axis=2)
    cw = jnp.concatenate([bdiag_c(c_re), -bdiag_c(c_im)], axis=1)
    are = jnp.broadcast_to(ab_re.reshape(1, g * p), (8, g * p))
    aim = jnp.broadcast_to(ab_im.reshape(1, g * p), (8, g * p))
    return are, aim, bw, cw


def _s5_common_specs(nb):
    return [
        pl.BlockSpec((nb, SSM_ST), lambda gb, t: (0, gb)),
        pl.BlockSpec((nb, SSM_ST), lambda gb, t: (0, gb)),
        pl.BlockSpec((8, SSM_ST), lambda gb, t: (0, gb)),
        pl.BlockSpec((8, SSM_ST), lambda gb, t: (0, gb)),
        pl.BlockSpec((None, SSM_CH, 2 * SSM_ST), lambda gb, t: (gb, 0, 0)),
        pl.BlockSpec((None, 2 * SSM_ST, SSM_CH), lambda gb, t: (gb, 0, 0)),
        pl.BlockSpec((1, SSM_CH), lambda gb, t: (0, gb)),
    ]


def _s5_prompt(h, nbatch, lp, params, dsk):
    assert nbatch == 4, "the paired scan packs two steps of 4 sequences into one 8-row tile"
    are, aim, bw, cw = params
    d = h.shape[1]
    ngb = d // SSM_CH
    tc = 256
    nt = lp // tc
    gp = are.shape[1]
    zeros = jnp.zeros((8, gp), F32)
    u_specs = [pl.BlockSpec((tc, SSM_CH), functools.partial(lambda gb, t, b: (b * nt + t, gb), b=b))
               for b in range(nbatch)]
    rows = tc * nbatch
    return pl.pallas_call(
        functools.partial(_s5_prompt_kernel, nbatch=nbatch, tc=tc),
        out_shape=(jax.ShapeDtypeStruct((nbatch, lp, d), BF16),
                   jax.ShapeDtypeStruct((8, gp), F32), jax.ShapeDtypeStruct((8, gp), F32)),
        grid=(ngb, nt),
        in_specs=u_specs + _s5_common_specs(8),
        out_specs=(pl.BlockSpec((nbatch, tc, SSM_CH), lambda gb, t: (0, t, gb)),
                   pl.BlockSpec((8, SSM_ST), lambda gb, t: (0, gb)),
                   pl.BlockSpec((8, SSM_ST), lambda gb, t: (0, gb))),
        scratch_shapes=[pltpu.VMEM((SSM_CH // 128, rows, 128), F32), pltpu.VMEM((rows, 2 * SSM_ST), F32),
                        pltpu.VMEM((8, 2 * SSM_ST), F32), pltpu.VMEM((SSM_CH // 128, rows, 128), F32)],
        compiler_params=_cparams(("parallel", "arbitrary")),
        name="s5_prompt",
    )(*([h] * nbatch), zeros, zeros, are, aim, bw, cw, dsk)


def _s5_sample(h, mp, bs, ls, h0re, h0im, params, dsk):
    are, aim, bw, cw = params
    d = h.shape[1]
    ngb = d // SSM_CH
    rows = ls * bs
    gp = are.shape[1]
    blk0 = mp // rows
    return pl.pallas_call(
        functools.partial(_s5_sample_kernel, nb=bs, tc=ls),
        out_shape=(jax.ShapeDtypeStruct((rows, d), BF16),
                   jax.ShapeDtypeStruct((bs, gp), F32), jax.ShapeDtypeStruct((bs, gp), F32)),
        grid=(ngb, 1),
        in_specs=[pl.BlockSpec((rows, SSM_CH), lambda gb, t: (blk0, gb))] + _s5_common_specs(bs),
        out_specs=(pl.BlockSpec((rows, SSM_CH), lambda gb, t: (0, gb)),
                   pl.BlockSpec((bs, SSM_ST), lambda gb, t: (0, gb)),
                   pl.BlockSpec((bs, SSM_ST), lambda gb, t: (0, gb))),
        scratch_shapes=[pltpu.VMEM((rows, 2 * SSM_ST), F32), pltpu.VMEM((bs, 2 * SSM_ST), F32)],
        compiler_params=_cparams(("parallel", "arbitrary")),
        name="s5_sample",
    )(h, h0re, h0im, are, aim, bw, cw, dsk)


def _ln_silu(y, g, b):
    mu = jnp.mean(y, axis=-1, keepdims=True)
    yc = y - mu
    z = yc * lax.rsqrt(jnp.mean(yc * yc, axis=-1, keepdims=True) + EPS) * g + b
    return _silu(z)


def _conv_prompt_kernel(cur_ref, halo_ref, buf_ref, w_ref, *rest, width, mode):
    if mode == "cf":
        bdw_ref, lng_ref, lnb_ref, o_ref, s_ref, acc_ref = rest
    else:
        bg_ref, o_ref, s_ref, acc_ref = rest
    t = pl.program_id(1)
    hb = halo_ref.shape[0]
    tc, d = cur_ref.shape
    first = t == 0
    s_ref[0:hb, :] = jnp.where(first, buf_ref[0], halo_ref[...])
    s_ref[hb:hb + tc, :] = cur_ref[...]
    off = hb - (width - 1)
    lch = 256
    for c in range(d // lch):
        cols = pl.ds(c * lch, lch)
        acc = w_ref[0:1, cols] * s_ref[off:off + tc, cols]
        for k in range(1, width):
            acc = acc + w_ref[k:k + 1, cols] * s_ref[off + k:off + k + tc, cols]
        acc_ref[:, cols] = acc
    if mode == "cf":
        o_ref[...] = _ln_silu(acc_ref[...] + bdw_ref[...], lng_ref[...], lnb_ref[...]).astype(o_ref.dtype)
    else:
        o_ref[...] = (bg_ref[...] * acc_ref[...]).astype(o_ref.dtype)


def _conv_prompt(v, nbatch, lp, w, layer, mode, extra):
    d = v.shape[1]
    width = w.shape[1]
    hb = 32 if width > 9 else 8
    tc = 256
    nt = lp // tc
    zeros = jnp.zeros((nbatch, hb, d), F32)
    in_specs = [
        pl.BlockSpec((tc, d), lambda b, t: (b * nt + t, 0)),
        pl.BlockSpec((hb, d), lambda b, t: (jnp.maximum((b * lp + t * tc) // hb - 1, 0), 0)),
        pl.BlockSpec((1, hb, d), lambda b, t: (b, 0, 0)),
        pl.BlockSpec((None, width, d), lambda b, t: (layer, 0, 0)),
    ]
    if mode == "cf":
        in_specs += [pl.BlockSpec((1, d), lambda b, t: (layer, 0))] * 3
    else:
        in_specs += [pl.BlockSpec((tc, d), lambda b, t: (b * nt + t, 0))]
    return pl.pallas_call(
        functools.partial(_conv_prompt_kernel, width=width, mode=mode),
        out_shape=jax.ShapeDtypeStruct((nbatch * lp, d), BF16),
        grid=(nbatch, nt),
        in_specs=in_specs,
        out_specs=pl.BlockSpec((tc, d), lambda b, t: (b * nt + t, 0)),
        scratch_shapes=[pltpu.VMEM((hb + tc, d), F32), pltpu.VMEM((tc, d), F32)],
        compiler_params=_cparams(("parallel", "arbitrary")),
        name="conv_prompt_" + mode,
    )(v, v, zeros, w, *extra)


def _conv_sample_kernel(v_ref, cache_ref, w_ref, *rest, width, mode):
    if mode == "cf":
        bdw_ref, lng_ref, lnb_ref, o_ref = rest
    else:
        bg_ref, o_ref = rest
    ls = v_ref.shape[0]
    hist = width - 1

    def full(j):
        return cache_ref[j] if j < hist else v_ref[j - hist]

    for l in range(ls):
        acc = w_ref[0:1, :] * full(l)
        for k in range(1, width):
            acc = acc + w_ref[k:k + 1, :] * full(l + k)
        if mode == "cf":
            o_ref[l] = _ln_silu(acc + bdw_ref[...], lng_ref[...], lnb_ref[...]).astype(o_ref.dtype)
        else:
            o_ref[l] = (bg_ref[l] * acc).astype(o_ref.dtype)


def _conv_sample(v_tm, cache_tm, w, layer, mode, extra):
    ls, bs, d = v_tm.shape
    width = w.shape[1]
    bc = 16
    in_specs = [
        pl.BlockSpec((ls, bc, d), lambda i: (0, i, 0)),
        pl.BlockSpec((width - 1, bc, d), lambda i: (0, i, 0)),
        pl.BlockSpec((None, width, d), lambda i: (layer, 0, 0)),
    ]
    if mode == "cf":
        in_specs += [pl.BlockSpec((1, d), lambda i: (layer, 0))] * 3
    else:
        in_specs += [pl.BlockSpec((ls, bc, d), lambda i: (0, i, 0))]
    return pl.pallas_call(
        functools.partial(_conv_sample_kernel, width=width, mode=mode),
        out_shape=jax.ShapeDtypeStruct((ls, bs, d), BF16),
        grid=(bs // bc,),
        in_specs=in_specs,
        out_specs=pl.BlockSpec((ls, bc, d), lambda i: (0, i, 0)),
        compiler_params=_cparams(("parallel",)),
        name="conv_sample_" + mode,
    )(v_tm, cache_tm, w, *extra)


def _onehot2(r):
    lane = lax.broadcasted_iota(I32, r.shape, 1).astype(F32)
    return jnp.where((lane == r[:, 0:1]) | (lane == r[:, 1:2]), 1.0, 0.0)


def _moe_plan_kernel(r_ref, pos_ref, info_ref, cnt_ref, run_ref, pst_ref):
    ph = pl.program_id(0)
    i = pl.program_id(1)
    r = r_ref[...]
    tb = r.shape[0]
    oh = _onehot2(r)
    colsum = jnp.broadcast_to(jnp.sum(oh, axis=0, keepdims=True), (8, ROUTE_LANES))

    @pl.when((ph == 0) & (i == 0))
    def _():
        cnt_ref[...] = jnp.zeros_like(cnt_ref)

    @pl.when(ph == 0)
    def _():
        cnt_ref[...] += colsum

    @pl.when((ph == 1) & (i == 0))
    def _():
        cnt = cnt_ref[...]
        nblk = jnp.floor((cnt + (MOE_BM - 1)) * (1.0 / MOE_BM))
        li = lax.broadcasted_iota(I32, (ROUTE_LANES, ROUTE_LANES), 0)
        lj = lax.broadcasted_iota(I32, (ROUTE_LANES, ROUTE_LANES), 1)
        upper = jnp.where(li <= lj, 1.0, 0.0).astype(BF16)
        bend = _dot(nblk.astype(BF16), upper)
        pst_ref[...] = (bend - nblk) * MOE_BM
        run_ref[...] = jnp.zeros_like(run_ref)
        n_used = bend[0:1, N_EXPERTS - 1:N_EXPERTS]
        bi = lax.broadcasted_iota(I32, (PLAN_ROWS, ROUTE_LANES), 0).astype(F32)
        ln = lax.broadcasted_iota(I32, (PLAN_ROWS, ROUTE_LANES), 1)
        ln_f = ln.astype(F32)
        pe = jnp.broadcast_to(bend[0:1, :], (PLAN_ROWS, ROUTE_LANES))
        pk = jnp.broadcast_to(nblk[0:1, :], (PLAN_ROWS, ROUTE_LANES))
        bcl = jnp.minimum(bi, n_used - 1.0)
        blk_e = jnp.sum(jnp.where((ln < N_EXPERTS) & (pe <= bcl), 1.0, 0.0), axis=1, keepdims=True)
        bend_col = jnp.sum(jnp.where(ln_f == bi, pe, 0.0), axis=1, keepdims=True)
        nblk_col = jnp.sum(jnp.where(ln_f == bi, pk, 0.0), axis=1, keepdims=True)
        info = jnp.where(ln == 0, blk_e, 0.0)
        info = jnp.where(ln == 1, n_used, info)
        info = jnp.where(ln == 2, bend_col, info)
        info = jnp.where(ln == 3, nblk_col, info)
        info_ref[...] = info.astype(I32)

    @pl.when(ph == 1)
    def _():
        ti = lax.broadcasted_iota(I32, (tb, tb), 0)
        tj = lax.broadcasted_iota(I32, (tb, tb), 1)
        lower = jnp.where(ti > tj, 1.0, 0.0).astype(BF16)
        before = _dot(lower, oh.astype(BF16))
        val = pst_ref[0:1, :] + run_ref[0:1, :] + before
        lane = lax.broadcasted_iota(I32, r.shape, 1)
        lane_f = lane.astype(F32)
        p0 = jnp.sum(jnp.where(lane_f == r[:, 0:1], val, 0.0), axis=1, keepdims=True)
        p1 = jnp.sum(jnp.where(lane_f == r[:, 1:2], val, 0.0), axis=1, keepdims=True)
        pos_ref[...] = jnp.where(lane == 0, p0, jnp.where(lane == 1, p1, 0.0)).astype(I32)
        run_ref[...] += colsum


def _moe_plan(route):
    m = route.shape[0]
    nb = m // PLAN_TB
    return pl.pallas_call(
        _moe_plan_kernel,
        out_shape=(jax.ShapeDtypeStruct((m, ROUTE_LANES), I32), jax.ShapeDtypeStruct((PLAN_ROWS, ROUTE_LANES), I32)),
        grid=(2, nb),
        in_specs=[pl.BlockSpec((PLAN_TB, ROUTE_LANES), lambda ph, i: (i, 0))],
        out_specs=(pl.BlockSpec((PLAN_TB, ROUTE_LANES), lambda ph, i: (ph * i, 0)),
                   pl.BlockSpec((PLAN_ROWS, ROUTE_LANES), lambda ph, i: (0, 0))),
        scratch_shapes=[pltpu.VMEM((8, ROUTE_LANES), F32)] * 3,
        compiler_params=_cparams(("arbitrary", "arbitrary")),
        name="moe_plan",
    )(route)


def _moe_scatter_kernel(pos_ref, bend_ref, nblk_ref, h_ref, xs_ref, z_ref, sem, zsem):
    i = pl.program_id(0)

    @pl.when(i == 0)
    def _():
        z_ref[...] = jnp.zeros_like(z_ref)

        def zero_copy(e):
            start = pl.multiple_of((bend_ref[e] - 1) * MOE_BM, MOE_BM)
            return pltpu.make_async_copy(z_ref, xs_ref.at[pl.ds(start, MOE_BM)], zsem)

        def start(e, c):
            @pl.when(nblk_ref[e] > 0)
            def _():
                zero_copy(e).start()
            return c

        def wait(e, c):
            @pl.when(nblk_ref[e] > 0)
            def _():
                zero_copy(e).wait()
            return c

        lax.fori_loop(0, N_EXPERTS, start, 0)
        lax.fori_loop(0, N_EXPERTS, wait, 0)

    def row(r, c):
        base = (i * TMC + r) * 2
        for k in range(2):
            p = pos_ref[base + k]
            pltpu.make_async_copy(h_ref.at[pl.ds(r, 1)], xs_ref.at[pl.ds(p, 1)], sem).start()
        return c

    lax.fori_loop(0, TMC, row, 0, unroll=4)
    for k in range(2):
        pltpu.make_async_copy(h_ref, xs_ref.at[pl.ds(0, TMC)], sem).wait()


def _moe_scatter(h, pos2, bend, nblk, n_rows):
    m, d = h.shape
    grid_spec = pltpu.PrefetchScalarGridSpec(
        num_scalar_prefetch=3,
        grid=(m // TMC,),
        in_specs=[pl.BlockSpec((TMC, d), lambda i, *_: (i, 0))],
        out_specs=pl.BlockSpec(memory_space=pl.ANY),
        scratch_shapes=[pltpu.VMEM((MOE_BM, d), F32), pltpu.SemaphoreType.DMA(()), pltpu.SemaphoreType.DMA(())],
    )
    return pl.pallas_call(
        _moe_scatter_kernel,
        out_shape=jax.ShapeDtypeStruct((n_rows, d), F32),
        grid_spec=grid_spec,
        compiler_params=_cparams(("arbitrary",)),
        name="moe_scatter",
    )(pos2, bend, nblk, h)


def _moe_expert_kernel(be_ref, nu_ref, xs_ref, w13_ref, w2_ref, y_ref, w13b_ref, w2b_ref):
    i = pl.program_id(0)
    used = i < nu_ref[0]
    new_expert = (i == 0) | (be_ref[i] != be_ref[jnp.maximum(i - 1, 0)])

    @pl.when(used & new_expert)
    def _():
        w13b_ref[...] = w13_ref[...].astype(BF16)
        w2b_ref[...] = w2_ref[...].astype(BF16)

    @pl.when(used)
    def _():
        ab = _dot(xs_ref[...].astype(BF16), w13b_ref[...])
        a = ab[:, 0:D_EXPERT]
        b = ab[:, D_EXPERT:2 * D_EXPERT]
        y_ref[...] = _dot((_silu(a) * b).astype(BF16), w2b_ref[...])

    @pl.when(jnp.logical_not(used))
    def _():
        y_ref[...] = jnp.zeros_like(y_ref)


def _moe_experts(xs, blk_e, n_used, w13, w2, layer):
    n_rows, d = xs.shape
    n_blocks = blk_e.shape[0]
    de2 = w13.shape[3]
    grid_spec = pltpu.PrefetchScalarGridSpec(
        num_scalar_prefetch=2,
        grid=(n_blocks,),
        in_specs=[
            pl.BlockSpec((MOE_BM, d), lambda i, be, nu: (jnp.minimum(i, nu[0] - 1), 0)),
            pl.BlockSpec((None, None, d, de2), lambda i, be, nu: (layer, be[i], 0, 0)),
            pl.BlockSpec((None, None, de2 // 2, d), lambda i, be, nu: (layer, be[i], 0, 0)),
        ],
        out_specs=pl.BlockSpec((MOE_BM, d), lambda i, be, nu: (i, 0)),
        scratch_shapes=[pltpu.VMEM((d, de2), BF16), pltpu.VMEM((de2 // 2, d), BF16)],
    )
    return pl.pallas_call(
        _moe_expert_kernel,
        out_shape=jax.ShapeDtypeStruct((n_rows, d), F32),
        grid_spec=grid_spec,
        compiler_params=_cparams(("arbitrary",)),
        name="moe_experts",
    )(blk_e, n_used, xs, w13, w2)


def _moe_combine_kernel(pos_ref, y_ref, x_ref, r_ref, gp_ref, gs_ref, o_ref, ybuf, sem, *, npb, bpb):
    i = pl.program_id(0)
    n = pl.num_programs(0)
    slot = i % 2

    def gather(blk, s):
        def row(r, c):
            base = (blk * TMC + r) * 2
            for k in range(2):
                p = pos_ref[base + k]
                pltpu.make_async_copy(y_ref.at[pl.ds(p, 1)], ybuf.at[s, k, pl.ds(r, 1)], sem.at[s]).start()
            return c
        lax.fori_loop(0, TMC, row, 0, unroll=4)

    @pl.when(i == 0)
    def _():
        gather(0, 0)

    @pl.when(i + 1 < n)
    def _():
        gather(i + 1, 1 - slot)

    for k in range(2):
        pltpu.make_async_copy(y_ref.at[pl.ds(0, TMC)], ybuf.at[slot, k], sem.at[slot]).wait()

    r = r_ref[...]
    f = r[:, 2:3] * ybuf[slot, 0] + r[:, 3:4] * ybuf[slot, 1]
    gate = _seq_vec(gp_ref, gs_ref, i, npb, bpb, TMC)
    o_ref[...] = x_ref[...] + gate * f


def _moe_combine(y, pos2, x, route, ap, as_, gate_idx, mp, lp):
    m, d = x.shape
    bs = as_.shape[0]
    grid_spec = pltpu.PrefetchScalarGridSpec(
        num_scalar_prefetch=1,
        grid=(m // TMC,),
        in_specs=[
            pl.BlockSpec(memory_space=pl.ANY),
            pl.BlockSpec((TMC, d), lambda i, pos: (i, 0)),
            pl.BlockSpec((TMC, ROUTE_LANES), lambda i, pos: (i, 0)),
            pl.BlockSpec((8, d), lambda i, pos: (0, gate_idx)),
            pl.BlockSpec((bs, d), lambda i, pos: (0, gate_idx)),
        ],
        out_specs=pl.BlockSpec((TMC, d), lambda i, pos: (i, 0)),
        scratch_shapes=[pltpu.VMEM((2, 2, TMC, d), F32), pltpu.SemaphoreType.DMA((2,))],
    )
    return pl.pallas_call(
        functools.partial(_moe_combine_kernel, npb=mp // TMC, bpb=lp // TMC),
        out_shape=jax.ShapeDtypeStruct((m, d), F32),
        grid_spec=grid_spec,
        compiler_params=_cparams(("arbitrary",)),
        name="moe_combine",
    )(pos2, y, x, route, ap, as_)


def _moe_layer(x, h2, route, w13, w2, layer, ap, as_, mp, lp):
    m = x.shape[0]
    n_blocks = (2 * m + N_EXPERTS * (MOE_BM - 1) + MOE_BM - 1) // MOE_BM
    assert n_blocks <= PLAN_ROWS and m % PLAN_TB == 0 and m % TMC == 0
    pos, info = _moe_plan(route)
    pos2 = pos[:, 0:2].reshape(2 * m)
    blk_e, n_used = info[:n_blocks, 0], info[0:1, 1]
    bend, nblk = info[:N_EXPERTS, 2], info[:N_EXPERTS, 3]
    xs = _moe_scatter(h2, pos2, bend, nblk, n_blocks * MOE_BM)
    y = _moe_experts(xs, blk_e, n_used, w13, w2, layer)
    return _moe_combine(y, pos2, x, route, ap, as_, 5, mp, lp)


def kernel(x_prompt, x_sample, c_prompt, c_sample, state_l0_ssm_re, state_l0_ssm_im, cache_l1_conformer_conv, cache_l2_short_conv, state_l3_ssm_re, state_l3_ssm_im, norm1_g, norm2_g, w_ada, b_ada, final_norm_g, ssm_a_re, ssm_a_im, ssm_log_dt, ssm_b_re, ssm_b_im, ssm_c_re, ssm_c_im, ssm_d, ssm_w_glu, cf_w_pw1, cf_w_dw, cf_b_dw, cf_ln_g, cf_ln_b, cf_w_pw2, sc_w_in, sc_w_conv, sc_w_out, moe_w_group, moe_b_group, moe_w_expert, moe_b_expert, moe_w13, moe_w2):
    bp, lp, d = x_prompt.shape
    bs, ls, _ = x_sample.shape
    depth = w_ada.shape[0]
    mp, ms = bp * lp, bs * ls
    g_ssm, p_ssm = ssm_a_re.shape[1:]
    assert d == D_MODEL and lp % TM == 0 and ms % TM == 0 and TM % bs == 0 and bp <= 8 and bs % 16 == 0
    assert TMC % bs == 0 and mp % ms == 0

    x = jnp.concatenate([x_prompt.reshape(mp, d), x_sample.transpose(1, 0, 2).reshape(ms, d)], axis=0)
    c_all = jnp.concatenate([c_prompt, jnp.zeros((8 - bp, d), F32), c_sample], axis=0)
    ada = _ada_all(c_all, w_ada, b_ada)

    sample_init = [(state_l0_ssm_re, state_l0_ssm_im), (cache_l1_conformer_conv,), (cache_l2_short_conv,),
                   (state_l3_ssm_re, state_l3_ssm_im)]
    p_states, s_states = [], []

    for i in range(depth):
        ap, as_ = ada[i, 0:8], ada[i, 8:]
        kind, j = i % 3, i // 3
        if kind == 0:
            h = _modulate(x, norm1_g[i:i + 1], ap, as_, 0, 1, mp, lp, F32)
            params = _s5_params(ssm_a_re[j], ssm_a_im[j], ssm_log_dt[j], ssm_b_re[j], ssm_b_im[j],
                                ssm_c_re[j], ssm_c_im[j])
            dsk = ssm_d[j:j + 1]
            pp, pre, pim = _s5_prompt(h, bp, lp, params, dsk)
            h0re, h0im = sample_init[i]
            psm, sre, sim = _s5_sample(h, mp, bs, ls, h0re.reshape(bs, g_ssm * p_ssm),
                                       h0im.reshape(bs, g_ssm * p_ssm), params, dsk)
            p_states.append((pre[:bp].reshape(bp, g_ssm, p_ssm), pim[:bp].reshape(bp, g_ssm, p_ssm)))
            s_states.append((sre.reshape(bs, g_ssm, p_ssm), sim.reshape(bs, g_ssm, p_ssm)))
            pre_out = (pp.reshape(mp, d), psm)
            w_out = ssm_w_glu
        elif kind == 1:
            h = _modulate(x, norm1_g[i:i + 1], ap, as_, 0, 1, mp, lp, BF16)
            gl = _mm_split((h,), cf_w_pw1, j, 2, _mm_glu_kernel, 1, 512, mp, "mm_glu")
            extra = (cf_b_dw, cf_ln_g, cf_ln_b)
            pp = _conv_prompt(gl, bp, lp, cf_w_dw, j, "cf", extra)
            (cache,) = sample_init[i]
            g_tm = gl[mp:].reshape(ls, bs, d)
            psm = _conv_sample(g_tm, cache.transpose(1, 0, 2), cf_w_dw, j, "cf", extra)
            hist = CF_WIDTH - 1
            p_states.append((jnp.stack([gl[b * lp + lp - hist:(b + 1) * lp] for b in range(bp)]),))
            s_states.append((jnp.concatenate([cache, g_tm.transpose(1, 0, 2)], axis=1)[:, -hist:],))
            pre_out = (pp, psm.reshape(ms, d))
            w_out = cf_w_pw2
        else:
            h = _modulate(x, norm1_g[i:i + 1], ap, as_, 0, 1, mp, lp, BF16)
            bg, v = _mm_split((h,), sc_w_in, j, 3, _mm_bcv_kernel, 2, 512, mp, "mm_bcv")
            pp = _conv_prompt(v, bp, lp, sc_w_conv, j, "sc", (bg,))
            (cache,) = sample_init[i]
            v_tm = v[mp:].reshape(ls, bs, d)
            psm = _conv_sample(v_tm, cache.transpose(1, 0, 2), sc_w_conv, j, "sc", (bg[mp:].reshape(ls, bs, d),))
            hist = SC_WIDTH - 1
            p_states.append((jnp.stack([v[b * lp + lp - hist:(b + 1) * lp] for b in range(bp)]),))
            s_states.append((jnp.concatenate([cache, v_tm.transpose(1, 0, 2)], axis=1)[:, -hist:],))
            pre_out = (pp, psm.reshape(ms, d))
            w_out = sc_w_out

        x = _mm_resid(pre_out, w_out, j, x, ap, as_, 2, mp, lp, glu=(kind == 0))

        wr = jnp.concatenate([moe_w_group[i], moe_w_expert[i],
                              jnp.zeros((d, ROUTE_LANES - N_GROUPS - N_EXPERTS), F32)], axis=1)
        br = jnp.concatenate([moe_b_group[i], moe_b_expert[i],
                              jnp.zeros((ROUTE_LANES - N_GROUPS - N_EXPERTS,), F32)]).reshape(1, ROUTE_LANES)
        h2, route = _modulate(x, norm2_g[i:i + 1], ap, as_, 3, 4, mp, lp, F32, router=(wr, br))
        x = _moe_layer(x, h2, route, moe_w13, moe_w2, i, ap, as_, mp, lp)

    y_p, y_s = _final_norm(x, final_norm_g.reshape(1, d), mp)
    y_prompt = y_p.reshape(bp, lp, d)
    y_sample = y_s.reshape(ls, bs, d).transpose(1, 0, 2)
    ps, ss = p_states, s_states
    return (y_prompt, y_sample,
            ps[0][0], ps[0][1], ps[1][0], ps[2][0], ps[3][0], ps[3][1],
            ss[0][0], ss[0][1], ss[1][0], ss[2][0], ss[3][0], ss[3][1])
```

```python
import functools

import jax
import jax.numpy as jnp
from jax import lax
from jax.experimental import pallas as pl
from jax.experimental.pallas import tpu as pltpu

F32 = jnp.float32
BF16 = jnp.bfloat16
I32 = jnp.int32

D_MODEL = 2048
SSM_GROUP = 16
SSM_STATE = 64
SSM_GB = 16
SSM_CH = SSM_GB * SSM_GROUP
SSM_ST = SSM_GB * SSM_STATE
CF_WIDTH = 31
SC_WIDTH = 3
N_GROUPS = 8
EPG = 8
N_EXPERTS = N_GROUPS * EPG
D_EXPERT = D_MODEL // 4
EPS = 1e-6

TM = 512
MOE_BM = 128
TMC = 256
PLAN_TB = 512
PLAN_ROWS = 256
ROUTE_LANES = 128
VMEM_LIMIT = 56 * 1024 * 1024


def _cparams(sem):
    return pltpu.CompilerParams(dimension_semantics=sem, vmem_limit_bytes=VMEM_LIMIT)


def _dot(a, b):
    return jnp.dot(a, b, preferred_element_type=F32)


def _dotb(a, b):
    return _dot(a.astype(BF16), b.astype(BF16))


def _sigmoid(x):
    return 1.0 / (1.0 + jnp.exp(-x))


def _silu(x):
    return x * _sigmoid(x)


def _gelu_tanh(x):
    return 0.5 * x * (1.0 + jnp.tanh(0.7978845608028654 * (x + 0.044715 * (x * x * x))))


def _seq_vec(vp_ref, vs_ref, blk, n_prompt_blocks, blocks_per_batch, rows):
    b = jnp.minimum(blk // blocks_per_batch, vp_ref.shape[0] - 1)
    vp = vp_ref[pl.ds(b, 1), :]
    vs = vs_ref[...]
    reps = rows // vs.shape[0]
    if reps > 1:
        vs = jnp.concatenate([vs] * reps, axis=0)
    return jnp.where(blk < n_prompt_blocks, vp, vs)


def _ada_kernel(c_ref, w_ref, b_ref, o_ref):
    c = c_ref[...]
    o_ref[0] = _dotb(_silu(c), w_ref[0]) + b_ref[0]


def _ada_all(c_all, w_ada, b_ada):
    depth, d, n = w_ada.shape
    rows = c_all.shape[0]
    tn = 1024
    return pl.pallas_call(
        _ada_kernel,
        out_shape=jax.ShapeDtypeStruct((depth, rows, n), F32),
        grid=(depth, n // tn),
        in_specs=[
            pl.BlockSpec((rows, d), lambda l, j: (0, 0)),
            pl.BlockSpec((1, d, tn), lambda l, j: (l, 0, j)),
            pl.BlockSpec((1, 1, tn), lambda l, j: (l, 0, j)),
        ],
        out_specs=pl.BlockSpec((1, rows, tn), lambda l, j: (l, 0, j)),
        compiler_params=_cparams(("parallel", "parallel")),
        name="ada_proj",
    )(c_all, w_ada, b_ada.reshape(depth, 1, n))


def _rms(x, g):
    return x * lax.rsqrt(jnp.mean(x * x, axis=-1, keepdims=True) + EPS) * g


def _modulate_kernel(x_ref, g_ref, shp_ref, scp_ref, shs_ref, scs_ref, o_ref, *, npb, bpb):
    i = pl.program_id(0)
    rows = x_ref.shape[0]
    y = _rms(x_ref[...], g_ref[...])
    sc = _seq_vec(scp_ref, scs_ref, i, npb, bpb, rows)
    sh = _seq_vec(shp_ref, shs_ref, i, npb, bpb, rows)
    o_ref[...] = (y * (1.0 + sc) + sh).astype(o_ref.dtype)


def _route_from_logits(lg):
    rows = lg.shape[0]
    lane = lax.broadcasted_iota(I32, (rows, ROUTE_LANES), 1)
    lane_f = lane.astype(F32)
    neg = jnp.float32(-jnp.inf)
    big = jnp.float32(ROUTE_LANES)
    is_g = lane < N_GROUPS
    gl = jnp.where(is_g, lg, neg)
    gmax = jnp.max(gl, axis=1, keepdims=True)
    grp = jnp.min(jnp.where(gl == gmax, lane_f, big), axis=1, keepdims=True)
    p_grp = 1.0 / jnp.sum(jnp.where(is_g, jnp.exp(gl - gmax), 0.0), axis=1, keepdims=True)
    e_grp = ((lane - N_GROUPS) >> 3).astype(F32)
    valid = (lane >= N_GROUPS) & (lane < N_GROUPS + N_EXPERTS) & (e_grp == grp)
    el = jnp.where(valid, lg, neg)
    v1 = jnp.max(el, axis=1, keepdims=True)
    i1 = jnp.min(jnp.where(el == v1, lane_f, big), axis=1, keepdims=True)
    el2 = jnp.where(lane_f == i1, neg, el)
    v2 = jnp.max(el2, axis=1, keepdims=True)
    i2 = jnp.min(jnp.where(el2 == v2, lane_f, big), axis=1, keepdims=True)
    t = jnp.exp(v2 - v1)
    den = 1.0 + t
    w1 = (1.0 / den) * p_grp
    w2 = (t / den) * p_grp
    out = jnp.where(lane == 0, i1 - N_GROUPS, 0.0)
    out = jnp.where(lane == 1, i2 - N_GROUPS, out)
    out = jnp.where(lane == 2, w1, out)
    out = jnp.where(lane == 3, w2, out)
    return out


def _modulate_route_kernel(x_ref, g_ref, shp_ref, scp_ref, shs_ref, scs_ref, wr_ref, br_ref,
                           o_ref, r_ref, *, npb, bpb):
    i = pl.program_id(0)
    rows = x_ref.shape[0]
    y = _rms(x_ref[...], g_ref[...])
    sc = _seq_vec(scp_ref, scs_ref, i, npb, bpb, rows)
    sh = _seq_vec(shp_ref, shs_ref, i, npb, bpb, rows)
    h = y * (1.0 + sc) + sh
    o_ref[...] = h
    lg = _dotb(h, wr_ref[...]) + br_ref[...]
    r_ref[...] = _route_from_logits(lg)


def _modulate(x, g, ap, as_, sh_idx, sc_idx, mp, lp, out_dtype, router=None):
    m, d = x.shape
    bs = as_.shape[0]
    npb, bpb = mp // TM, lp // TM
    in_specs = [
        pl.BlockSpec((TM, d), lambda i: (i, 0)),
        pl.BlockSpec((1, d), lambda i: (0, 0)),
        pl.BlockSpec((8, d), lambda i: (0, sh_idx)),
        pl.BlockSpec((8, d), lambda i: (0, sc_idx)),
        pl.BlockSpec((bs, d), lambda i: (0, sh_idx)),
        pl.BlockSpec((bs, d), lambda i: (0, sc_idx)),
    ]
    if router is None:
        return pl.pallas_call(
            functools.partial(_modulate_kernel, npb=npb, bpb=bpb),
            out_shape=jax.ShapeDtypeStruct((m, d), out_dtype),
            grid=(m // TM,),
            in_specs=in_specs,
            out_specs=pl.BlockSpec((TM, d), lambda i: (i, 0)),
            compiler_params=_cparams(("parallel",)),
            name="modulate",
        )(x, g, ap, ap, as_, as_)
    wr, br = router
    return pl.pallas_call(
        functools.partial(_modulate_route_kernel, npb=npb, bpb=bpb),
        out_shape=(jax.ShapeDtypeStruct((m, d), F32), jax.ShapeDtypeStruct((m, ROUTE_LANES), F32)),
        grid=(m // TM,),
        in_specs=in_specs + [
            pl.BlockSpec((d, ROUTE_LANES), lambda i: (0, 0)),
            pl.BlockSpec((1, ROUTE_LANES), lambda i: (0, 0)),
        ],
        out_specs=(pl.BlockSpec((TM, d), lambda i: (i, 0)), pl.BlockSpec((TM, ROUTE_LANES), lambda i: (i, 0))),
        compiler_params=_cparams(("parallel",)),
        name="modulate_route",
    )(x, g, ap, ap, as_, as_, wr, br)


def _final_norm_kernel(x_ref, g_ref, op_ref, os_ref, *, npb):
    i = pl.program_id(0)
    y = _rms(x_ref[...], g_ref[...])

    @pl.when(i < npb)
    def _():
        op_ref[...] = y

    @pl.when(i >= npb)
    def _():
        os_ref[...] = y


def _final_norm(x, g, mp):
    m, d = x.shape
    npb = mp // TM
    return pl.pallas_call(
        functools.partial(_final_norm_kernel, npb=npb),
        out_shape=(jax.ShapeDtypeStruct((mp, d), F32), jax.ShapeDtypeStruct((m - mp, d), F32)),
        grid=(m // TM,),
        in_specs=[pl.BlockSpec((TM, d), lambda i: (i, 0)), pl.BlockSpec((1, d), lambda i: (0, 0))],
        out_specs=(pl.BlockSpec((TM, d), lambda i: (jnp.minimum(i, npb - 1), 0)),
                   pl.BlockSpec((TM, d), lambda i: (jnp.maximum(i - npb, 0), 0))),
        compiler_params=_cparams(("arbitrary",)),
        name="final_norm",
    )(x, g)


def _a_specs(a_parts, npb):
    k = a_parts[0].shape[1]
    if len(a_parts) == 1:
        return [pl.BlockSpec((TM, k), lambda j, i: (i, 0))]
    return [pl.BlockSpec((TM, k), lambda j, i: (jnp.minimum(i, npb - 1), 0)),
            pl.BlockSpec((TM, k), lambda j, i: (jnp.maximum(i - npb, 0), 0))]


def _mm_products(a_refs, w_refs, wbf_ref, npb):
    i = pl.program_id(1)

    @pl.when(i == 0)
    def _():
        for s, w_ref in enumerate(w_refs):
            wbf_ref[s] = w_ref[...].astype(BF16)

    if len(a_refs) == 1:
        a = a_refs[0][...]
    else:
        a = jnp.where(i < npb, a_refs[0][...], a_refs[1][...])
    a = a.astype(BF16)
    return [_dot(a, wbf_ref[s]) for s in range(len(w_refs))]


def _mm_glu_kernel(*refs, n_a, npb):
    a_refs, (wa_ref, wb_ref, o_ref, wbf_ref) = refs[:n_a], refs[n_a:]
    za, zb = _mm_products(a_refs, (wa_ref, wb_ref), wbf_ref, npb)
    o_ref[...] = za * _sigmoid(zb)


def _mm_bcv_kernel(*refs, n_a, npb):
    a_refs, (w0_ref, w1_ref, w2_ref, bg_ref, v_ref, wbf_ref) = refs[:n_a], refs[n_a:]
    bg, cg, hv = _mm_products(a_refs, (w0_ref, w1_ref, w2_ref), wbf_ref, npb)
    bg_ref[...] = bg
    v_ref[...] = cg * hv


def _mm_resid_kernel(*refs, n_a, npb, bpb, glu):
    a_refs, rest = refs[:n_a], refs[n_a:]
    n_w = 2 if glu else 1
    w_refs = rest[:n_w]
    x_ref, gp_ref, gs_ref, o_ref, wbf_ref = rest[n_w:]
    z = _mm_products(a_refs, w_refs, wbf_ref, npb)
    z = z[0] * _sigmoid(z[1]) if glu else z[0]
    gate = _seq_vec(gp_ref, gs_ref, pl.program_id(1), npb, bpb, x_ref.shape[0])
    o_ref[...] = x_ref[...] + gate * z


def _w_spec(k, tn, layer, col0):
    return pl.BlockSpec((None, k, tn), lambda j, i: (layer, 0, col0 + j))


def _mm_split(a_parts, w, layer, n_split, kernel_fn, n_out, tn, mp, name):
    m = sum(a.shape[0] for a in a_parts)
    k = a_parts[0].shape[1]
    n = w.shape[2] // n_split
    nj = n // tn
    out_shape = [jax.ShapeDtypeStruct((m, n), F32) for _ in range(n_out)]
    out_specs = [pl.BlockSpec((TM, tn), lambda j, i: (i, j)) for _ in range(n_out)]
    return pl.pallas_call(
        functools.partial(kernel_fn, n_a=len(a_parts), npb=mp // TM),
        out_shape=out_shape if n_out > 1 else out_shape[0],
        grid=(nj, m // TM),
        in_specs=_a_specs(a_parts, mp // TM) + [_w_spec(k, tn, layer, s * nj) for s in range(n_split)],
        out_specs=out_specs if n_out > 1 else out_specs[0],
        scratch_shapes=[pltpu.VMEM((n_split, k, tn), BF16)],
        compiler_params=_cparams(("parallel", "arbitrary")),
        name=name,
    )(*a_parts, *([w] * n_split))


def _mm_resid(a_parts, w, layer, x, ap, as_, gate_idx, mp, lp, glu=False):
    m = x.shape[0]
    k = a_parts[0].shape[1]
    n_split = 2 if glu else 1
    n = w.shape[2] // n_split
    tn = 512 if glu else 1024
    nj = n // tn
    bs = as_.shape[0]
    return pl.pallas_call(
        functools.partial(_mm_resid_kernel, n_a=len(a_parts), npb=mp // TM, bpb=lp // TM, glu=glu),
        out_shape=jax.ShapeDtypeStruct((m, n), F32),
        grid=(nj, m // TM),
        in_specs=_a_specs(a_parts, mp // TM)
        + [_w_spec(k, tn, layer, s * nj) for s in range(n_split)]
        + [
            pl.BlockSpec((TM, tn), lambda j, i: (i, j)),
            pl.BlockSpec((8, tn), lambda j, i: (0, gate_idx * nj + j)),
            pl.BlockSpec((bs, tn), lambda j, i: (0, gate_idx * nj + j)),
        ],
        out_specs=pl.BlockSpec((TM, tn), lambda j, i: (i, j)),
        scratch_shapes=[pltpu.VMEM((n_split, k, tn), BF16)],
        compiler_params=_cparams(("parallel", "arbitrary")),
        name="mm_glu_resid" if glu else "mm_resid",
    )(*a_parts, *([w] * n_split), x, ap, as_)


S5_LC = 512


def _s5_scan(bu_ref, st_ref, are_ref, aim_ref, nb, tc):
    n_tiles = nb // 8

    def tile(r, carry):
        r0 = pl.multiple_of(r * 8, 8)
        for c in range(SSM_ST // S5_LC):
            cre = pl.ds(c * S5_LC, S5_LC)
            cim = pl.ds(SSM_ST + c * S5_LC, S5_LC)
            ar = are_ref[:, cre]
            ai = aim_ref[:, cre]
            srow = pl.ds(r0, 8)
            hr = st_ref[srow, cre]
            hi = st_ref[srow, cim]

            def step(t, carry):
                hr, hi = carry
                row = pl.ds(pl.multiple_of(t * nb + r0, 8), 8)
                nr = ar * hr - ai * hi + bu_ref[row, cre]
                ni = ar * hi + ai * hr + bu_ref[row, cim]
                bu_ref[row, cre] = nr
                bu_ref[row, cim] = ni
                return nr, ni

            hr, hi = lax.fori_loop(0, tc, step, (hr, hi), unroll=8)
            st_ref[srow, cre] = hr
            st_ref[srow, cim] = hi
        return carry

    lax.fori_loop(0, n_tiles, tile, 0)


def _s5_core(u, bu_ref, st_ref, h0re_ref, h0im_ref, are_ref, aim_ref, bw_ref, cw_ref, dsk_ref,
             sre_ref, sim_ref, scan):
    t = pl.program_id(1)

    @pl.when(t == 0)
    def _():
        st_ref[:, 0:SSM_ST] = h0re_ref[...]
        st_ref[:, SSM_ST:2 * SSM_ST] = h0im_ref[...]

    bu_ref[...] = _dotb(u, bw_ref[...])
    scan(bu_ref, st_ref, are_ref, aim_ref)
    y = _dot(bu_ref[...].astype(BF16), cw_ref[...].astype(BF16)) + dsk_ref[...] * u

    @pl.when(t == pl.num_programs(1) - 1)
    def _():
        sre_ref[...] = st_ref[:, 0:SSM_ST]
        sim_ref[...] = st_ref[:, SSM_ST:2 * SSM_ST]

    return _gelu_tanh(y)


S5_NSUB = 4


def _s5_scan_pairs_sub(bu_ref, h_ref, ar, ai, carry):
    n_lc = len(ar)
    first_half = lax.broadcasted_iota(I32, (8, S5_LC), 0) < 4
    for j in range(bu_ref.shape[0] // 8):
        rows = slice(j * 8, j * 8 + 8)
        new = []
        for c in range(n_lc):
            cre = slice(c * S5_LC, (c + 1) * S5_LC)
            cim = slice(SSM_ST + c * S5_LC, SSM_ST + (c + 1) * S5_LC)
            hr, hi = carry[c]
            br = bu_ref[rows, cre]
            bi = bu_ref[rows, cim]
            er = ar[c] * hr - ai[c] * hi + br
            ei = ar[c] * hi + ai[c] * hr + bi
            sr = pltpu.roll(er, 4, 0)
            si = pltpu.roll(ei, 4, 0)
            orr = ar[c] * sr - ai[c] * si + br
            oi = ar[c] * si + ai[c] * sr + bi
            h_ref[rows, cre] = jnp.where(first_half, er, orr)
            h_ref[rows, cim] = jnp.where(first_half, ei, oi)
            new.append((pltpu.roll(orr, 4, 0), pltpu.roll(oi, 4, 0)))
        carry = new
    return carry


def _s5_prompt_kernel(*refs, nbatch, tc):
    u_refs = refs[:nbatch]
    (h0re_ref, h0im_ref, are_ref, aim_ref, bw_ref, cw_ref, dsk_ref,
     p_ref, sre_ref, sim_ref, r_ref, st_ref, y_ref) = refs[nbatch:nbatch + 13]
    bu_refs = refs[nbatch + 13:nbatch + 13 + S5_NSUB]
    h_refs = refs[nbatch + 13 + S5_NSUB:]
    t = pl.program_id(1)

    @pl.when(t == 0)
    def _():
        st_ref[:, 0:SSM_ST] = h0re_ref[...]
        st_ref[:, SSM_ST:2 * SSM_ST] = h0im_ref[...]

    nslab = r_ref.shape[0]
    for b in range(nbatch):
        ub = u_refs[b][...]
        for s in range(nslab):
            r_ref[s, pl.ds(b, tc, stride=nbatch), :] = ub[:, s * 128:(s + 1) * 128]
    u = jnp.concatenate([r_ref[s] for s in range(nslab)], axis=1)
    u_bf = u.astype(BF16)
    bw = bw_ref[...].astype(BF16)
    cw = cw_ref[...].astype(BF16)
    sub = u.shape[0] // S5_NSUB
    n_lc = SSM_ST // S5_LC
    ar = [are_ref[:, c * S5_LC:(c + 1) * S5_LC] for c in range(n_lc)]
    ai = [aim_ref[:, c * S5_LC:(c + 1) * S5_LC] for c in range(n_lc)]
    carry = [(st_ref[:, c * S5_LC:(c + 1) * S5_LC], st_ref[:, SSM_ST + c * S5_LC:SSM_ST + (c + 1) * S5_LC])
             for c in range(n_lc)]

    def b_proj(s):
        bu_refs[s][...] = _dot(u_bf[s * sub:(s + 1) * sub], bw)

    def c_proj(s):
        return _dot(h_refs[s][...].astype(BF16), cw)

    ys = []
    b_proj(0)
    for s in range(S5_NSUB):
        if s + 1 < S5_NSUB:
            b_proj(s + 1)
        carry = _s5_scan_pairs_sub(bu_refs[s], h_refs[s], ar, ai, carry)
        if s >= 1:
            ys.append(c_proj(s - 1))
    ys.append(c_proj(S5_NSUB - 1))
    for c in range(n_lc):
        st_ref[:, c * S5_LC:(c + 1) * S5_LC] = carry[c][0]
        st_ref[:, SSM_ST + c * S5_LC:SSM_ST + (c + 1) * S5_LC] = carry[c][1]
    y = _gelu_tanh(jnp.concatenate(ys, axis=0) + dsk_ref[...] * u)
    for s in range(nslab):
        y_ref[s] = y[:, s * 128:(s + 1) * 128]
    for b in range(nbatch):
        p_ref[b] = jnp.concatenate([y_ref[s, pl.ds(b, tc, stride=nbatch), :] for s in range(nslab)],
                                   axis=1).astype(p_ref.dtype)

    @pl.when(t == pl.num_programs(1) - 1)
    def _():
        sre_ref[...] = st_ref[:, 0:SSM_ST]
        sim_ref[...] = st_ref[:, SSM_ST:2 * SSM_ST]


def _s5_sample_kernel(u_ref, h0re_ref, h0im_ref, are_ref, aim_ref, bw_ref, cw_ref, dsk_ref,
                      p_ref, sre_ref, sim_ref, bu_ref, st_ref, *, nb, tc):
    scan = functools.partial(_s5_scan, nb=nb, tc=tc)
    p_ref[...] = _s5_core(u_ref[...], bu_ref, st_ref, h0re_ref, h0im_ref, are_ref, aim_ref, bw_ref, cw_ref,
                          dsk_ref, sre_ref, sim_ref, scan).astype(p_ref.dtype)


def _s5_params(a_re, a_im, log_dt, b_re, b_im, c_re, c_im):
    g, p = a_re.shape
    dt = jnp.exp(log_dt)[:, None]
    mag = jnp.exp(dt * a_re)
    ang = dt * a_im
    ab_re = mag * jnp.cos(ang)
    ab_im = mag * jnp.sin(ang)
    n_re = ab_re - 1.0
    n_im = ab_im
    den = a_re * a_re + a_im * a_im
    f_re = ((n_re * a_re + n_im * a_im) / den)[..., None]
    f_im = ((n_im * a_re - n_re * a_im) / den)[..., None]
    bb_re = f_re * b_re - f_im * b_im
    bb_im = f_re * b_im + f_im * b_re
    ngb = g // SSM_GB
    eye = jnp.eye(SSM_GB, dtype=F32)

    def bdiag_b(bb):
        t = bb.reshape(ngb, SSM_GB, p, SSM_GROUP).transpose(0, 1, 3, 2)
        t = t[:, :, :, None, :] * eye[None, :, None, :, None]
        return t.reshape(ngb, SSM_CH, SSM_ST)

    def bdiag_c(cc):
        t = cc.reshape(ngb, SSM_GB, SSM_GROUP, p).transpose(0, 1, 3, 2)
        t = t[:, :, :, None, :] * eye[None, :, None, :, None]
        return t.reshape(ngb, SSM_ST, SSM_CH)

    bw = jnp.concatenate([bdiag_b(bb_re), bdiag_b(bb_im)], axis=2)
    cw = jnp.concatenate([bdiag_c(c_re), -bdiag_c(c_im)], axis=1)
    are = jnp.broadcast_to(ab_re.reshape(1, g * p), (8, g * p))
    aim = jnp.broadcast_to(ab_im.reshape(1, g * p), (8, g * p))
    return are, aim, bw, cw


def _s5_common_specs(nb):
    return [
        pl.BlockSpec((nb, SSM_ST), lambda gb, t: (0, gb)),
        pl.BlockSpec((nb, SSM_ST), lambda gb, t: (0, gb)),
        pl.BlockSpec((8, SSM_ST), lambda gb, t: (0, gb)),
        pl.BlockSpec((8, SSM_ST), lambda gb, t: (0, gb)),
        pl.BlockSpec((None, SSM_CH, 2 * SSM_ST), lambda gb, t: (gb, 0, 0)),
        pl.BlockSpec((None, 2 * SSM_ST, SSM_CH), lambda gb, t: (gb, 0, 0)),
        pl.BlockSpec((1, SSM_CH), lambda gb, t: (0, gb)),
    ]


def _s5_prompt(h, nbatch, lp, params, dsk):
    assert nbatch == 4, "the paired scan packs two steps of 4 sequences into one 8-row tile"
    are, aim, bw, cw = params
    d = h.shape[1]
    ngb = d // SSM_CH
    tc = 256
    nt = lp // tc
    gp = are.shape[1]
    zeros = jnp.zeros((8, gp), F32)
    u_specs = [pl.BlockSpec((tc, SSM_CH), functools.partial(lambda gb, t, b: (b * nt + t, gb), b=b))
               for b in range(nbatch)]
    rows = tc * nbatch
    return pl.pallas_call(
        functools.partial(_s5_prompt_kernel, nbatch=nbatch, tc=tc),
        out_shape=(jax.ShapeDtypeStruct((nbatch, lp, d), BF16),
                   jax.ShapeDtypeStruct((8, gp), F32), jax.ShapeDtypeStruct((8, gp), F32)),
        grid=(ngb, nt),
        in_specs=u_specs + _s5_common_specs(8),
        out_specs=(pl.BlockSpec((nbatch, tc, SSM_CH), lambda gb, t: (0, t, gb)),
                   pl.BlockSpec((8, SSM_ST), lambda gb, t: (0, gb)),
                   pl.BlockSpec((8, SSM_ST), lambda gb, t: (0, gb))),
        scratch_shapes=[pltpu.VMEM((SSM_CH // 128, rows, 128), F32), pltpu.VMEM((8, 2 * SSM_ST), F32),
                        pltpu.VMEM((SSM_CH // 128, rows, 128), F32)]
        + [pltpu.VMEM((rows // S5_NSUB, 2 * SSM_ST), F32)] * (2 * S5_NSUB),
        compiler_params=_cparams(("parallel", "arbitrary")),
        name="s5_prompt",
    )(*([h] * nbatch), zeros, zeros, are, aim, bw, cw, dsk)


def _s5_sample(h, mp, bs, ls, h0re, h0im, params, dsk):
    are, aim, bw, cw = params
    d = h.shape[1]
    ngb = d // SSM_CH
    rows = ls * bs
    gp = are.shape[1]
    blk0 = mp // rows
    return pl.pallas_call(
        functools.partial(_s5_sample_kernel, nb=bs, tc=ls),
        out_shape=(jax.ShapeDtypeStruct((rows, d), BF16),
                   jax.ShapeDtypeStruct((bs, gp), F32), jax.ShapeDtypeStruct((bs, gp), F32)),
        grid=(ngb, 1),
        in_specs=[pl.BlockSpec((rows, SSM_CH), lambda gb, t: (blk0, gb))] + _s5_common_specs(bs),
        out_specs=(pl.BlockSpec((rows, SSM_CH), lambda gb, t: (0, gb)),
                   pl.BlockSpec((bs, SSM_ST), lambda gb, t: (0, gb)),
                   pl.BlockSpec((bs, SSM_ST), lambda gb, t: (0, gb))),
        scratch_shapes=[pltpu.VMEM((rows, 2 * SSM_ST), F32), pltpu.VMEM((bs, 2 * SSM_ST), F32)],
        compiler_params=_cparams(("parallel", "arbitrary")),
        name="s5_sample",
    )(h, h0re, h0im, are, aim, bw, cw, dsk)


def _ln_silu(y, g, b):
    mu = jnp.mean(y, axis=-1, keepdims=True)
    yc = y - mu
    z = yc * lax.rsqrt(jnp.mean(yc * yc, axis=-1, keepdims=True) + EPS) * g + b
    return _silu(z)


def _conv_prompt_kernel(cur_ref, halo_ref, buf_ref, w_ref, *rest, width, mode):
    if mode == "cf":
        bdw_ref, lng_ref, lnb_ref, o_ref, s_ref, acc_ref = rest
    else:
        bg_ref, o_ref, s_ref, acc_ref = rest
    t = pl.program_id(1)
    hb = halo_ref.shape[0]
    tc, d = cur_ref.shape
    first = t == 0
    s_ref[0:hb, :] = jnp.where(first, buf_ref[0], halo_ref[...])
    s_ref[hb:hb + tc, :] = cur_ref[...]
    s_ref[hb + tc:hb + tc + 8, :] = jnp.zeros((8, d), F32)
    off = hb - (width - 1)
    by_shift = [[(o // 8, o - off) for o in range(off, off + width) if o % 8 == sh] for sh in range(8)]
    rc, lch = 64, 256
    for r0 in range(0, tc, rc):
        for c in range(d // lch):
            cols = pl.ds(c * lch, lch)
            out = None
            for sh, taps in enumerate(by_shift):
                z = None
                for q, k in taps:
                    t = w_ref[k:k + 1, cols] * s_ref[r0 + 8 * q:r0 + 8 * q + rc + 8, cols]
                    z = t if z is None else z + t
                if z is not None:
                    zs = z[sh:sh + rc]
                    out = zs if out is None else out + zs
            acc_ref[r0:r0 + rc, cols] = out
    if mode == "cf":
        o_ref[...] = _ln_silu(acc_ref[...] + bdw_ref[...], lng_ref[...], lnb_ref[...]).astype(o_ref.dtype)
    else:
        o_ref[...] = (bg_ref[...] * acc_ref[...]).astype(o_ref.dtype)


def _conv_prompt(v, nbatch, lp, w, layer, mode, extra):
    d = v.shape[1]
    width = w.shape[1]
    hb = 32 if width > 9 else 8
    tc = 256
    nt = lp // tc
    zeros = jnp.zeros((nbatch, hb, d), F32)
    in_specs = [
        pl.BlockSpec((tc, d), lambda b, t: (b * nt + t, 0)),
        pl.BlockSpec((hb, d), lambda b, t: (jnp.maximum((b * lp + t * tc) // hb - 1, 0), 0)),
        pl.BlockSpec((1, hb, d), lambda b, t: (b, 0, 0)),
        pl.BlockSpec((None, width, d), lambda b, t: (layer, 0, 0)),
    ]
    if mode == "cf":
        in_specs += [pl.BlockSpec((1, d), lambda b, t: (layer, 0))] * 3
    else:
        in_specs += [pl.BlockSpec((tc, d), lambda b, t: (b * nt + t, 0))]
    return pl.pallas_call(
        functools.partial(_conv_prompt_kernel, width=width, mode=mode),
        out_shape=jax.ShapeDtypeStruct((nbatch * lp, d), BF16),
        grid=(nbatch, nt),
        in_specs=in_specs,
        out_specs=pl.BlockSpec((tc, d), lambda b, t: (b * nt + t, 0)),
        scratch_shapes=[pltpu.VMEM((hb + tc + 8, d), F32), pltpu.VMEM((tc, d), F32)],
        compiler_params=_cparams(("parallel", "arbitrary")),
        name="conv_prompt_" + mode,
    )(v, v, zeros, w, *extra)


def _conv_sample_kernel(v_ref, cache_ref, w_ref, *rest, width, mode):
    if mode == "cf":
        bdw_ref, lng_ref, lnb_ref, o_ref = rest
    else:
        bg_ref, o_ref = rest
    ls = v_ref.shape[0]
    hist = width - 1

    def full(j):
        return cache_ref[j] if j < hist else v_ref[j - hist]

    for l in range(ls):
        acc = w_ref[0:1, :] * full(l)
        for k in range(1, width):
            acc = acc + w_ref[k:k + 1, :] * full(l + k)
        if mode == "cf":
            o_ref[l] = _ln_silu(acc + bdw_ref[...], lng_ref[...], lnb_ref[...]).astype(o_ref.dtype)
        else:
            o_ref[l] = (bg_ref[l] * acc).astype(o_ref.dtype)


def _conv_sample(v_tm, cache_tm, w, layer, mode, extra):
    ls, bs, d = v_tm.shape
    width = w.shape[1]
    bc = 16
    in_specs = [
        pl.BlockSpec((ls, bc, d), lambda i: (0, i, 0)),
        pl.BlockSpec((width - 1, bc, d), lambda i: (0, i, 0)),
        pl.BlockSpec((None, width, d), lambda i: (layer, 0, 0)),
    ]
    if mode == "cf":
        in_specs += [pl.BlockSpec((1, d), lambda i: (layer, 0))] * 3
    else:
        in_specs += [pl.BlockSpec((ls, bc, d), lambda i: (0, i, 0))]
    return pl.pallas_call(
        functools.partial(_conv_sample_kernel, width=width, mode=mode),
        out_shape=jax.ShapeDtypeStruct((ls, bs, d), BF16),
        grid=(bs // bc,),
        in_specs=in_specs,
        out_specs=pl.BlockSpec((ls, bc, d), lambda i: (0, i, 0)),
        compiler_params=_cparams(("parallel",)),
        name="conv_sample_" + mode,
    )(v_tm, cache_tm, w, *extra)


def _onehot2(r):
    lane = lax.broadcasted_iota(I32, r.shape, 1).astype(F32)
    return jnp.where((lane == r[:, 0:1]) | (lane == r[:, 1:2]), 1.0, 0.0)


def _moe_plan_kernel(r_ref, pos_ref, info_ref, cnt_ref, run_ref, pst_ref):
    ph = pl.program_id(0)
    i = pl.program_id(1)
    r = r_ref[...]
    tb = r.shape[0]
    oh = _onehot2(r)
    colsum = jnp.broadcast_to(jnp.sum(oh, axis=0, keepdims=True), (8, ROUTE_LANES))

    @pl.when((ph == 0) & (i == 0))
    def _():
        cnt_ref[...] = jnp.zeros_like(cnt_ref)

    @pl.when(ph == 0)
    def _():
        cnt_ref[...] += colsum

    @pl.when((ph == 1) & (i == 0))
    def _():
        cnt = cnt_ref[...]
        nblk = jnp.floor((cnt + (MOE_BM - 1)) * (1.0 / MOE_BM))
        li = lax.broadcasted_iota(I32, (ROUTE_LANES, ROUTE_LANES), 0)
        lj = lax.broadcasted_iota(I32, (ROUTE_LANES, ROUTE_LANES), 1)
        upper = jnp.where(li <= lj, 1.0, 0.0).astype(BF16)
        bend = _dot(nblk.astype(BF16), upper)
        pst_ref[...] = (bend - nblk) * MOE_BM
        run_ref[...] = jnp.zeros_like(run_ref)
        n_used = bend[0:1, N_EXPERTS - 1:N_EXPERTS]
        bi = lax.broadcasted_iota(I32, (PLAN_ROWS, ROUTE_LANES), 0).astype(F32)
        ln = lax.broadcasted_iota(I32, (PLAN_ROWS, ROUTE_LANES), 1)
        ln_f = ln.astype(F32)
        pe = jnp.broadcast_to(bend[0:1, :], (PLAN_ROWS, ROUTE_LANES))
        pk = jnp.broadcast_to(nblk[0:1, :], (PLAN_ROWS, ROUTE_LANES))
        bcl = jnp.minimum(bi, n_used - 1.0)
        blk_e = jnp.sum(jnp.where((ln < N_EXPERTS) & (pe <= bcl), 1.0, 0.0), axis=1, keepdims=True)
        bend_col = jnp.sum(jnp.where(ln_f == bi, pe, 0.0), axis=1, keepdims=True)
        nblk_col = jnp.sum(jnp.where(ln_f == bi, pk, 0.0), axis=1, keepdims=True)
        info = jnp.where(ln == 0, blk_e, 0.0)
        info = jnp.where(ln == 1, n_used, info)
        info = jnp.where(ln == 2, bend_col, info)
        info = jnp.where(ln == 3, nblk_col, info)
        nonempty = (ln < N_EXPERTS) & (pk > 0.0)
        seq = jnp.sum(jnp.where(nonempty & (ln_f < blk_e), 1.0, 0.0), axis=1, keepdims=True)
        nxt = jnp.min(jnp.where(nonempty & (ln_f > blk_e), ln_f, float(ROUTE_LANES)), axis=1, keepdims=True)
        info = jnp.where(ln == 4, seq, info)
        info = jnp.where(ln == 5, nxt, info)
        info_ref[...] = info.astype(I32)

    @pl.when(ph == 1)
    def _():
        ti = lax.broadcasted_iota(I32, (tb, tb), 0)
        tj = lax.broadcasted_iota(I32, (tb, tb), 1)
        lower = jnp.where(ti > tj, 1.0, 0.0).astype(BF16)
        before = _dot(lower, oh.astype(BF16))
        val = pst_ref[0:1, :] + run_ref[0:1, :] + before
        lane = lax.broadcasted_iota(I32, r.shape, 1)
        lane_f = lane.astype(F32)
        p0 = jnp.sum(jnp.where(lane_f == r[:, 0:1], val, 0.0), axis=1, keepdims=True)
        p1 = jnp.sum(jnp.where(lane_f == r[:, 1:2], val, 0.0), axis=1, keepdims=True)
        pos_ref[...] = jnp.where(lane == 0, p0, jnp.where(lane == 1, p1, 0.0)).astype(I32)
        run_ref[...] += colsum


def _moe_plan(route):
    m = route.shape[0]
    nb = m // PLAN_TB
    return pl.pallas_call(
        _moe_plan_kernel,
        out_shape=(jax.ShapeDtypeStruct((m, ROUTE_LANES), I32), jax.ShapeDtypeStruct((PLAN_ROWS, ROUTE_LANES), I32)),
        grid=(2, nb),
        in_specs=[pl.BlockSpec((PLAN_TB, ROUTE_LANES), lambda ph, i: (i, 0))],
        out_specs=(pl.BlockSpec((PLAN_TB, ROUTE_LANES), lambda ph, i: (ph * i, 0)),
                   pl.BlockSpec((PLAN_ROWS, ROUTE_LANES), lambda ph, i: (0, 0))),
        scratch_shapes=[pltpu.VMEM((8, ROUTE_LANES), F32)] * 3,
        compiler_params=_cparams(("arbitrary", "arbitrary")),
        name="moe_plan",
    )(route)


def _moe_scatter_kernel(pos_ref, bend_ref, nblk_ref, nu_ref, h_ref, xs_ref, z_ref, sem, zsem):
    i = pl.program_id(0)
    n_blocks = xs_ref.shape[0] // MOE_BM

    @pl.when(i == 0)
    def _():
        z_ref[...] = jnp.zeros_like(z_ref)

        def zero_copy(blk):
            start = pl.multiple_of(blk * MOE_BM, MOE_BM)
            return pltpu.make_async_copy(z_ref, xs_ref.at[pl.ds(start, MOE_BM)], zsem)

        def for_each_pad_block(fn):
            def per_expert(e, c):
                @pl.when(nblk_ref[e] > 0)
                def _():
                    fn(zero_copy(bend_ref[e] - 1))
                return c

            def per_tail_block(blk, c):
                @pl.when(blk >= nu_ref[0])
                def _():
                    fn(zero_copy(blk))
                return c

            lax.fori_loop(0, N_EXPERTS, per_expert, 0)
            lax.fori_loop(0, n_blocks, per_tail_block, 0)

        for_each_pad_block(lambda c: c.start())
        for_each_pad_block(lambda c: c.wait())

    def row(r, c):
        base = (i * TMC + r) * 2
        for k in range(2):
            p = pos_ref[base + k]
            pltpu.make_async_copy(h_ref.at[pl.ds(r, 1)], xs_ref.at[pl.ds(p, 1)], sem).start()
        return c

    lax.fori_loop(0, TMC, row, 0, unroll=4)
    for k in range(2):
        pltpu.make_async_copy(h_ref, xs_ref.at[pl.ds(0, TMC)], sem).wait()


def _moe_scatter(h, pos2, bend, nblk, n_used, n_rows):
    m, d = h.shape
    grid_spec = pltpu.PrefetchScalarGridSpec(
        num_scalar_prefetch=4,
        grid=(m // TMC,),
        in_specs=[pl.BlockSpec((TMC, d), lambda i, *_: (i, 0))],
        out_specs=pl.BlockSpec(memory_space=pl.ANY),
        scratch_shapes=[pltpu.VMEM((MOE_BM, d), F32), pltpu.SemaphoreType.DMA(()), pltpu.SemaphoreType.DMA(())],
    )
    return pl.pallas_call(
        _moe_scatter_kernel,
        out_shape=jax.ShapeDtypeStruct((n_rows, d), F32),
        grid_spec=grid_spec,
        compiler_params=_cparams(("arbitrary",)),
        name="moe_scatter",
    )(pos2, bend, nblk, n_used, h)


def _moe_expert_kernel(be_ref, nu_ref, seq_ref, nxt_ref, xs_ref, w13_hbm, w2_hbm, y_ref,
                       w13f_ref, w2f_ref, w13b_ref, w2b_ref, sem13, sem2, *, layer):
    i = pl.program_id(0)
    used = i < nu_ref[0]
    e = be_ref[i]
    new_expert = (i == 0) | (e != be_ref[jnp.maximum(i - 1, 0)])
    slot = seq_ref[i] % 2

    def fetch(expert, s):
        return (pltpu.make_async_copy(w13_hbm.at[layer, expert], w13f_ref.at[s], sem13.at[s]),
                pltpu.make_async_copy(w2_hbm.at[layer, expert], w2f_ref.at[s], sem2.at[s]))

    @pl.when(i == 0)
    def _():
        for c in fetch(e, slot):
            c.start()

    @pl.when(used & new_expert)
    def _():
        nxt = nxt_ref[i]

        @pl.when(nxt < N_EXPERTS)
        def _():
            for c in fetch(nxt, 1 - slot):
                c.start()

        for c in fetch(e, slot):
            c.wait()
        w13b_ref[...] = w13f_ref[slot].astype(BF16)
        w2b_ref[...] = w2f_ref[slot].astype(BF16)

    @pl.when(used)
    def _():
        ab = _dot(xs_ref[...].astype(BF16), w13b_ref[...])
        a = ab[:, 0:D_EXPERT]
        b = ab[:, D_EXPERT:2 * D_EXPERT]
        y_ref[...] = _dot((_silu(a) * b).astype(BF16), w2b_ref[...])

    @pl.when(jnp.logical_not(used))
    def _():
        y_ref[...] = jnp.zeros_like(y_ref)


def _moe_experts(xs, blk_e, n_used, seq, nxt, w13, w2, layer):
    n_rows, d = xs.shape
    n_blocks = blk_e.shape[0]
    de2 = w13.shape[3]
    grid_spec = pltpu.PrefetchScalarGridSpec(
        num_scalar_prefetch=4,
        grid=(n_blocks,),
        in_specs=[
            pl.BlockSpec((MOE_BM, d), lambda i, be, nu, sq, nx: (jnp.minimum(i, nu[0] - 1), 0)),
            pl.BlockSpec(memory_space=pl.ANY),
            pl.BlockSpec(memory_space=pl.ANY),
        ],
        out_specs=pl.BlockSpec((MOE_BM, d), lambda i, be, nu, sq, nx: (i, 0)),
        scratch_shapes=[pltpu.VMEM((2, d, de2), F32), pltpu.VMEM((2, de2 // 2, d), F32),
                        pltpu.VMEM((d, de2), BF16), pltpu.VMEM((de2 // 2, d), BF16),
                        pltpu.SemaphoreType.DMA((2,)), pltpu.SemaphoreType.DMA((2,))],
    )
    return pl.pallas_call(
        functools.partial(_moe_expert_kernel, layer=layer),
        out_shape=jax.ShapeDtypeStruct((n_rows, d), F32),
        grid_spec=grid_spec,
        compiler_params=_cparams(("arbitrary",)),
        name="moe_experts",
    )(blk_e, n_used, seq, nxt, xs, w13, w2)


def _moe_combine_kernel(pos_ref, y_ref, x_ref, r_ref, gp_ref, gs_ref, o_ref, ybuf, sem, *, npb, bpb):
    i = pl.program_id(0)
    n = pl.num_programs(0)
    slot = i % 2

    def gather(blk, s):
        def row(r, c):
            base = (blk * TMC + r) * 2
            for k in range(2):
                p = pos_ref[base + k]
                pltpu.make_async_copy(y_ref.at[pl.ds(p, 1)], ybuf.at[s, k, pl.ds(r, 1)], sem.at[s]).start()
            return c
        lax.fori_loop(0, TMC, row, 0, unroll=4)

    @pl.when(i == 0)
    def _():
        gather(0, 0)

    @pl.when(i + 1 < n)
    def _():
        gather(i + 1, 1 - slot)

    for k in range(2):
        pltpu.make_async_copy(y_ref.at[pl.ds(0, TMC)], ybuf.at[slot, k], sem.at[slot]).wait()

    r = r_ref[...]
    f = r[:, 2:3] * ybuf[slot, 0] + r[:, 3:4] * ybuf[slot, 1]
    gate = _seq_vec(gp_ref, gs_ref, i, npb, bpb, TMC)
    o_ref[...] = x_ref[...] + gate * f


def _moe_combine(y, pos2, x, route, ap, as_, gate_idx, mp, lp):
    m, d = x.shape
    bs = as_.shape[0]
    grid_spec = pltpu.PrefetchScalarGridSpec(
        num_scalar_prefetch=1,
        grid=(m // TMC,),
        in_specs=[
            pl.BlockSpec(memory_space=pl.ANY),
            pl.BlockSpec((TMC, d), lambda i, pos: (i, 0)),
            pl.BlockSpec((TMC, ROUTE_LANES), lambda i, pos: (i, 0)),
            pl.BlockSpec((8, d), lambda i, pos: (0, gate_idx)),
            pl.BlockSpec((bs, d), lambda i, pos: (0, gate_idx)),
        ],
        out_specs=pl.BlockSpec((TMC, d), lambda i, pos: (i, 0)),
        scratch_shapes=[pltpu.VMEM((2, 2, TMC, d), F32), pltpu.SemaphoreType.DMA((2,))],
    )
    return pl.pallas_call(
        functools.partial(_moe_combine_kernel, npb=mp // TMC, bpb=lp // TMC),
        out_shape=jax.ShapeDtypeStruct((m, d), F32),
        grid_spec=grid_spec,
        compiler_params=_cparams(("arbitrary",)),
        name="moe_combine",
    )(pos2, y, x, route, ap, as_)


def _moe_layer(x, h2, route, w13, w2, layer, ap, as_, mp, lp):
    m = x.shape[0]
    n_blocks = (2 * m + N_EXPERTS * (MOE_BM - 1) + MOE_BM - 1) // MOE_BM
    assert n_blocks <= PLAN_ROWS and m % PLAN_TB == 0 and m % TMC == 0
    pos, info = _moe_plan(route)
    pos2 = pos[:, 0:2].reshape(2 * m)
    blk_e, n_used = info[:n_blocks, 0], info[0:1, 1]
    bend, nblk = info[:N_EXPERTS, 2], info[:N_EXPERTS, 3]
    seq, nxt = info[:n_blocks, 4], info[:n_blocks, 5]
    xs = _moe_scatter(h2, pos2, bend, nblk, n_used, n_blocks * MOE_BM)
    y = _moe_experts(xs, blk_e, n_used, seq, nxt, w13, w2, layer)
    return _moe_combine(y, pos2, x, route, ap, as_, 5, mp, lp)


def kernel(x_prompt, x_sample, c_prompt, c_sample, state_l0_ssm_re, state_l0_ssm_im, cache_l1_conformer_conv, cache_l2_short_conv, state_l3_ssm_re, state_l3_ssm_im, norm1_g, norm2_g, w_ada, b_ada, final_norm_g, ssm_a_re, ssm_a_im, ssm_log_dt, ssm_b_re, ssm_b_im, ssm_c_re, ssm_c_im, ssm_d, ssm_w_glu, cf_w_pw1, cf_w_dw, cf_b_dw, cf_ln_g, cf_ln_b, cf_w_pw2, sc_w_in, sc_w_conv, sc_w_out, moe_w_group, moe_b_group, moe_w_expert, moe_b_expert, moe_w13, moe_w2):
    bp, lp, d = x_prompt.shape
    bs, ls, _ = x_sample.shape
    depth = w_ada.shape[0]
    mp, ms = bp * lp, bs * ls
    g_ssm, p_ssm = ssm_a_re.shape[1:]
    assert d == D_MODEL and lp % TM == 0 and ms % TM == 0 and TM % bs == 0 and bp <= 8 and bs % 16 == 0
    assert TMC % bs == 0 and mp % ms == 0

    x = jnp.concatenate([x_prompt.reshape(mp, d), x_sample.transpose(1, 0, 2).reshape(ms, d)], axis=0)
    c_all = jnp.concatenate([c_prompt, jnp.zeros((8 - bp, d), F32), c_sample], axis=0)
    ada = _ada_all(c_all, w_ada, b_ada)
    n_pad = ROUTE_LANES - N_GROUPS - N_EXPERTS
    wr_all = jnp.concatenate([moe_w_group, moe_w_expert, jnp.zeros((depth, d, n_pad), F32)], axis=2)
    br_all = jnp.concatenate([moe_b_group, moe_b_expert, jnp.zeros((depth, n_pad), F32)], axis=1)

    sample_init = [(state_l0_ssm_re, state_l0_ssm_im), (cache_l1_conformer_conv,), (cache_l2_short_conv,),
                   (state_l3_ssm_re, state_l3_ssm_im)]
    p_states, s_states = [], []

    for i in range(depth):
        ap, as_ = ada[i, 0:8], ada[i, 8:]
        kind, j = i % 3, i // 3
        if kind == 0:
            h = _modulate(x, norm1_g[i:i + 1], ap, as_, 0, 1, mp, lp, F32)
            params = _s5_params(ssm_a_re[j], ssm_a_im[j], ssm_log_dt[j], ssm_b_re[j], ssm_b_im[j],
                                ssm_c_re[j], ssm_c_im[j])
            dsk = ssm_d[j:j + 1]
            pp, pre, pim = _s5_prompt(h, bp, lp, params, dsk)
            h0re, h0im = sample_init[i]
            psm, sre, sim = _s5_sample(h, mp, bs, ls, h0re.reshape(bs, g_ssm * p_ssm),
                                       h0im.reshape(bs, g_ssm * p_ssm), params, dsk)
            p_states.append((pre[:bp].reshape(bp, g_ssm, p_ssm), pim[:bp].reshape(bp, g_ssm, p_ssm)))
            s_states.append((sre.reshape(bs, g_ssm, p_ssm), sim.reshape(bs, g_ssm, p_ssm)))
            pre_out = (pp.reshape(mp, d), psm)
            w_out = ssm_w_glu
        elif kind == 1:
            h = _modulate(x, norm1_g[i:i + 1], ap, as_, 0, 1, mp, lp, BF16)
            gl = _mm_split((h,), cf_w_pw1, j, 2, _mm_glu_kernel, 1, 512, mp, "mm_glu")
            extra = (cf_b_dw, cf_ln_g, cf_ln_b)
            pp = _conv_prompt(gl, bp, lp, cf_w_dw, j, "cf", extra)
            (cache,) = sample_init[i]
            g_tm = gl[mp:].reshape(ls, bs, d)
            psm = _conv_sample(g_tm, cache.transpose(1, 0, 2), cf_w_dw, j, "cf", extra)
            hist = CF_WIDTH - 1
            p_states.append((jnp.stack([gl[b * lp + lp - hist:(b + 1) * lp] for b in range(bp)]),))
            s_states.append((jnp.concatenate([cache, g_tm.transpose(1, 0, 2)], axis=1)[:, -hist:],))
            pre_out = (pp, psm.reshape(ms, d))
            w_out = cf_w_pw2
        else:
            h = _modulate(x, norm1_g[i:i + 1], ap, as_, 0, 1, mp, lp, BF16)
            bg, v = _mm_split((h,), sc_w_in, j, 3, _mm_bcv_kernel, 2, 512, mp, "mm_bcv")
            pp = _conv_prompt(v, bp, lp, sc_w_conv, j, "sc", (bg,))
            (cache,) = sample_init[i]
            v_tm = v[mp:].reshape(ls, bs, d)
            psm = _conv_sample(v_tm, cache.transpose(1, 0, 2), sc_w_conv, j, "sc", (bg[mp:].reshape(ls, bs, d),))
            hist = SC_WIDTH - 1
            p_states.append((jnp.stack([v[b * lp + lp - hist:(b + 1) * lp] for b in range(bp)]),))
            s_states.append((jnp.concatenate([cache, v_tm.transpose(1, 0, 2)], axis=1)[:, -hist:],))
            pre_out = (pp, psm.reshape(ms, d))
            w_out = sc_w_out

        x = _mm_resid(pre_out, w_out, j, x, ap, as_, 2, mp, lp, glu=(kind == 0))

        h2, route = _modulate(x, norm2_g[i:i + 1], ap, as_, 3, 4, mp, lp, F32, router=(wr_all[i], br_all[i:i + 1]))
        x = _moe_layer(x, h2, route, moe_w13, moe_w2, i, ap, as_, mp, lp)

    y_p, y_s = _final_norm(x, final_norm_g.reshape(1, d), mp)
    y_prompt = y_p.reshape(bp, lp, d)
    y_sample = y_s.reshape(ls, bs, d).transpose(1, 0, 2)
    ps, ss = p_states, s_states
    return (y_prompt, y_sample,
            ps[0][0], ps[0][1], ps[1][0], ps[2][0], ps[3][0], ps[3][1],
            ss[0][0], ss[0][1], ss[1][0], ss[2][0], ss[3][0], ss[3][1])
```

```python
import functools

import jax
import jax.numpy as jnp
from jax import lax
from jax.experimental import pallas as pl
from jax.experimental.pallas import tpu as pltpu

F32 = jnp.float32
BF16 = jnp.bfloat16
I32 = jnp.int32

D_MODEL = 2048
SSM_GROUP = 16
SSM_STATE = 64
SSM_GB = 16
SSM_CH = SSM_GB * SSM_GROUP
SSM_ST = SSM_GB * SSM_STATE
CF_WIDTH = 31
SC_WIDTH = 3
N_GROUPS = 8
EPG = 8
N_EXPERTS = N_GROUPS * EPG
D_EXPERT = D_MODEL // 4
EPS = 1e-6

TM = 512
MOE_BM = 128
TMC = 256
PLAN_TB = 512
PLAN_ROWS = 256
ROUTE_LANES = 128
VMEM_LIMIT = 56 * 1024 * 1024


def _cparams(sem):
    return pltpu.CompilerParams(dimension_semantics=sem, vmem_limit_bytes=VMEM_LIMIT)


def _dot(a, b):
    return jnp.dot(a, b, preferred_element_type=F32)


def _dotb(a, b):
    return _dot(a.astype(BF16), b.astype(BF16))


def _sigmoid(x):
    return 1.0 / (1.0 + jnp.exp(-x))


def _silu(x):
    return x * _sigmoid(x)


def _gelu_tanh(x):
    return 0.5 * x * (1.0 + jnp.tanh(0.7978845608028654 * (x + 0.044715 * (x * x * x))))


def _seq_vec(vp_ref, vs_ref, blk, n_prompt_blocks, blocks_per_batch, rows):
    b = jnp.minimum(blk // blocks_per_batch, vp_ref.shape[0] - 1)
    vp = vp_ref[pl.ds(b, 1), :]
    vs = vs_ref[...]
    reps = rows // vs.shape[0]
    if reps > 1:
        vs = jnp.concatenate([vs] * reps, axis=0)
    return jnp.where(blk < n_prompt_blocks, vp, vs)


def _ada_kernel(c_ref, w_ref, b_ref, o_ref):
    c = c_ref[...]
    o_ref[0] = _dotb(_silu(c), w_ref[0]) + b_ref[0]


def _ada_all(c_all, w_ada, b_ada):
    depth, d, n = w_ada.shape
    rows = c_all.shape[0]
    tn = 1024
    return pl.pallas_call(
        _ada_kernel,
        out_shape=jax.ShapeDtypeStruct((depth, rows, n), F32),
        grid=(depth, n // tn),
        in_specs=[
            pl.BlockSpec((rows, d), lambda l, j: (0, 0)),
            pl.BlockSpec((1, d, tn), lambda l, j: (l, 0, j)),
            pl.BlockSpec((1, 1, tn), lambda l, j: (l, 0, j)),
        ],
        out_specs=pl.BlockSpec((1, rows, tn), lambda l, j: (l, 0, j)),
        compiler_params=_cparams(("parallel", "parallel")),
        name="ada_proj",
    )(c_all, w_ada, b_ada.reshape(depth, 1, n))


def _rms(x, g):
    return x * lax.rsqrt(jnp.mean(x * x, axis=-1, keepdims=True) + EPS) * g


def _modulate_kernel(x_ref, g_ref, shp_ref, scp_ref, shs_ref, scs_ref, o_ref, *, npb, bpb):
    i = pl.program_id(0)
    rows = x_ref.shape[0]
    y = _rms(x_ref[...], g_ref[...])
    sc = _seq_vec(scp_ref, scs_ref, i, npb, bpb, rows)
    sh = _seq_vec(shp_ref, shs_ref, i, npb, bpb, rows)
    o_ref[...] = (y * (1.0 + sc) + sh).astype(o_ref.dtype)


def _route_from_logits(lg):
    rows = lg.shape[0]
    lane = lax.broadcasted_iota(I32, (rows, ROUTE_LANES), 1)
    lane_f = lane.astype(F32)
    neg = jnp.float32(-jnp.inf)
    big = jnp.float32(ROUTE_LANES)
    is_g = lane < N_GROUPS
    gl = jnp.where(is_g, lg, neg)
    gmax = jnp.max(gl, axis=1, keepdims=True)
    grp = jnp.min(jnp.where(gl == gmax, lane_f, big), axis=1, keepdims=True)
    p_grp = 1.0 / jnp.sum(jnp.where(is_g, jnp.exp(gl - gmax), 0.0), axis=1, keepdims=True)
    e_grp = ((lane - N_GROUPS) >> 3).astype(F32)
    valid = (lane >= N_GROUPS) & (lane < N_GROUPS + N_EXPERTS) & (e_grp == grp)
    el = jnp.where(valid, lg, neg)
    v1 = jnp.max(el, axis=1, keepdims=True)
    i1 = jnp.min(jnp.where(el == v1, lane_f, big), axis=1, keepdims=True)
    el2 = jnp.where(lane_f == i1, neg, el)
    v2 = jnp.max(el2, axis=1, keepdims=True)
    i2 = jnp.min(jnp.where(el2 == v2, lane_f, big), axis=1, keepdims=True)
    t = jnp.exp(v2 - v1)
    den = 1.0 + t
    w1 = (1.0 / den) * p_grp
    w2 = (t / den) * p_grp
    out = jnp.where(lane == 0, i1 - N_GROUPS, 0.0)
    out = jnp.where(lane == 1, i2 - N_GROUPS, out)
    out = jnp.where(lane == 2, w1, out)
    out = jnp.where(lane == 3, w2, out)
    return out


def _modulate_route_kernel(x_ref, g_ref, shp_ref, scp_ref, shs_ref, scs_ref, wr_ref, br_ref,
                           o_ref, r_ref, *, npb, bpb):
    i = pl.program_id(0)
    rows = x_ref.shape[0]
    y = _rms(x_ref[...], g_ref[...])
    sc = _seq_vec(scp_ref, scs_ref, i, npb, bpb, rows)
    sh = _seq_vec(shp_ref, shs_ref, i, npb, bpb, rows)
    h = y * (1.0 + sc) + sh
    o_ref[...] = h
    lg = _dotb(h, wr_ref[...]) + br_ref[...]
    r_ref[...] = _route_from_logits(lg)


def _modulate(x, g, ap, as_, sh_idx, sc_idx, mp, lp, out_dtype, router=None):
    m, d = x.shape
    bs = as_.shape[0]
    npb, bpb = mp // TM, lp // TM
    in_specs = [
        pl.BlockSpec((TM, d), lambda i: (i, 0)),
        pl.BlockSpec((1, d), lambda i: (0, 0)),
        pl.BlockSpec((8, d), lambda i: (0, sh_idx)),
        pl.BlockSpec((8, d), lambda i: (0, sc_idx)),
        pl.BlockSpec((bs, d), lambda i: (0, sh_idx)),
        pl.BlockSpec((bs, d), lambda i: (0, sc_idx)),
    ]
    if router is None:
        return pl.pallas_call(
            functools.partial(_modulate_kernel, npb=npb, bpb=bpb),
            out_shape=jax.ShapeDtypeStruct((m, d), out_dtype),
            grid=(m // TM,),
            in_specs=in_specs,
            out_specs=pl.BlockSpec((TM, d), lambda i: (i, 0)),
            compiler_params=_cparams(("parallel",)),
            name="modulate",
        )(x, g, ap, ap, as_, as_)
    wr, br = router
    return pl.pallas_call(
        functools.partial(_modulate_route_kernel, npb=npb, bpb=bpb),
        out_shape=(jax.ShapeDtypeStruct((m, d), F32), jax.ShapeDtypeStruct((m, ROUTE_LANES), F32)),
        grid=(m // TM,),
        in_specs=in_specs + [
            pl.BlockSpec((d, ROUTE_LANES), lambda i: (0, 0)),
            pl.BlockSpec((1, ROUTE_LANES), lambda i: (0, 0)),
        ],
        out_specs=(pl.BlockSpec((TM, d), lambda i: (i, 0)), pl.BlockSpec((TM, ROUTE_LANES), lambda i: (i, 0))),
        compiler_params=_cparams(("parallel",)),
        name="modulate_route",
    )(x, g, ap, ap, as_, as_, wr, br)


def _final_norm_kernel(x_ref, g_ref, op_ref, os_ref, *, npb):
    i = pl.program_id(0)
    y = _rms(x_ref[...], g_ref[...])

    @pl.when(i < npb)
    def _():
        op_ref[...] = y

    @pl.when(i >= npb)
    def _():
        os_ref[...] = y


def _final_norm(x, g, mp):
    m, d = x.shape
    npb = mp // TM
    return pl.pallas_call(
        functools.partial(_final_norm_kernel, npb=npb),
        out_shape=(jax.ShapeDtypeStruct((mp, d), F32), jax.ShapeDtypeStruct((m - mp, d), F32)),
        grid=(m // TM,),
        in_specs=[pl.BlockSpec((TM, d), lambda i: (i, 0)), pl.BlockSpec((1, d), lambda i: (0, 0))],
        out_specs=(pl.BlockSpec((TM, d), lambda i: (jnp.minimum(i, npb - 1), 0)),
                   pl.BlockSpec((TM, d), lambda i: (jnp.maximum(i - npb, 0), 0))),
        compiler_params=_cparams(("arbitrary",)),
        name="final_norm",
    )(x, g)


def _a_specs(a_parts, npb):
    k = a_parts[0].shape[1]
    if len(a_parts) == 1:
        return [pl.BlockSpec((TM, k), lambda j, i: (i, 0))]
    return [pl.BlockSpec((TM, k), lambda j, i: (jnp.minimum(i, npb - 1), 0)),
            pl.BlockSpec((TM, k), lambda j, i: (jnp.maximum(i - npb, 0), 0))]


def _mm_products(a_refs, w_refs, wbf_ref, npb):
    i = pl.program_id(1)

    @pl.when(i == 0)
    def _():
        for s, w_ref in enumerate(w_refs):
            wbf_ref[s] = w_ref[...].astype(BF16)

    if len(a_refs) == 1:
        a = a_refs[0][...]
    else:
        a = jnp.where(i < npb, a_refs[0][...], a_refs[1][...])
    a = a.astype(BF16)
    return [_dot(a, wbf_ref[s]) for s in range(len(w_refs))]


def _mm_glu_kernel(*refs, n_a, npb):
    a_refs, (wa_ref, wb_ref, o_ref, wbf_ref) = refs[:n_a], refs[n_a:]
    za, zb = _mm_products(a_refs, (wa_ref, wb_ref), wbf_ref, npb)
    o_ref[...] = za * _sigmoid(zb)


def _mm_bcv_kernel(*refs, n_a, npb):
    a_refs, (w0_ref, w1_ref, w2_ref, bg_ref, v_ref, wbf_ref) = refs[:n_a], refs[n_a:]
    bg, cg, hv = _mm_products(a_refs, (w0_ref, w1_ref, w2_ref), wbf_ref, npb)
    bg_ref[...] = bg
    v_ref[...] = cg * hv


def _mm_resid_kernel(*refs, n_a, npb, bpb, glu):
    a_refs, rest = refs[:n_a], refs[n_a:]
    n_w = 2 if glu else 1
    w_refs = rest[:n_w]
    x_ref, gp_ref, gs_ref, o_ref, wbf_ref = rest[n_w:]
    z = _mm_products(a_refs, w_refs, wbf_ref, npb)
    z = z[0] * _sigmoid(z[1]) if glu else z[0]
    gate = _seq_vec(gp_ref, gs_ref, pl.program_id(1), npb, bpb, x_ref.shape[0])
    o_ref[...] = x_ref[...] + gate * z


def _w_spec(k, tn, layer, col0):
    return pl.BlockSpec((None, k, tn), lambda j, i: (layer, 0, col0 + j))


def _mm_split(a_parts, w, layer, n_split, kernel_fn, n_out, tn, mp, name):
    m = sum(a.shape[0] for a in a_parts)
    k = a_parts[0].shape[1]
    n = w.shape[2] // n_split
    nj = n // tn
    out_shape = [jax.ShapeDtypeStruct((m, n), F32) for _ in range(n_out)]
    out_specs = [pl.BlockSpec((TM, tn), lambda j, i: (i, j)) for _ in range(n_out)]
    return pl.pallas_call(
        functools.partial(kernel_fn, n_a=len(a_parts), npb=mp // TM),
        out_shape=out_shape if n_out > 1 else out_shape[0],
        grid=(nj, m // TM),
        in_specs=_a_specs(a_parts, mp // TM) + [_w_spec(k, tn, layer, s * nj) for s in range(n_split)],
        out_specs=out_specs if n_out > 1 else out_specs[0],
        scratch_shapes=[pltpu.VMEM((n_split, k, tn), BF16)],
        compiler_params=_cparams(("parallel", "arbitrary")),
        name=name,
    )(*a_parts, *([w] * n_split))


def _mm_resid(a_parts, w, layer, x, ap, as_, gate_idx, mp, lp, glu=False):
    m = x.shape[0]
    k = a_parts[0].shape[1]
    n_split = 2 if glu else 1
    n = w.shape[2] // n_split
    tn = 512 if glu else 1024
    nj = n // tn
    bs = as_.shape[0]
    return pl.pallas_call(
        functools.partial(_mm_resid_kernel, n_a=len(a_parts), npb=mp // TM, bpb=lp // TM, glu=glu),
        out_shape=jax.ShapeDtypeStruct((m, n), F32),
        grid=(nj, m // TM),
        in_specs=_a_specs(a_parts, mp // TM)
        + [_w_spec(k, tn, layer, s * nj) for s in range(n_split)]
        + [
            pl.BlockSpec((TM, tn), lambda j, i: (i, j)),
            pl.BlockSpec((8, tn), lambda j, i: (0, gate_idx * nj + j)),
            pl.BlockSpec((bs, tn), lambda j, i: (0, gate_idx * nj + j)),
        ],
        out_specs=pl.BlockSpec((TM, tn), lambda j, i: (i, j)),
        scratch_shapes=[pltpu.VMEM((n_split, k, tn), BF16)],
        compiler_params=_cparams(("parallel", "arbitrary")),
        name="mm_glu_resid" if glu else "mm_resid",
    )(*a_parts, *([w] * n_split), x, ap, as_)


S5_LC = 512


def _s5_scan(bu_ref, st_ref, are_ref, aim_ref, nb, tc):
    n_tiles = nb // 8

    def tile(r, carry):
        r0 = pl.multiple_of(r * 8, 8)
        for c in range(SSM_ST // S5_LC):
            cre = pl.ds(c * S5_LC, S5_LC)
            cim = pl.ds(SSM_ST + c * S5_LC, S5_LC)
            ar = are_ref[:, cre]
            ai = aim_ref[:, cre]
            srow = pl.ds(r0, 8)
            hr = st_ref[srow, cre]
            hi = st_ref[srow, cim]

            def step(t, carry):
                hr, hi = carry
                row = pl.ds(pl.multiple_of(t * nb + r0, 8), 8)
                nr = ar * hr - ai * hi + bu_ref[row, cre]
                ni = ar * hi + ai * hr + bu_ref[row, cim]
                bu_ref[row, cre] = nr
                bu_ref[row, cim] = ni
                return nr, ni

            hr, hi = lax.fori_loop(0, tc, step, (hr, hi), unroll=8)
            st_ref[srow, cre] = hr
            st_ref[srow, cim] = hi
        return carry

    lax.fori_loop(0, n_tiles, tile, 0)


def _s5_core(u, bu_ref, st_ref, h0re_ref, h0im_ref, are_ref, aim_ref, bw_ref, cw_ref, dsk_ref,
             sre_ref, sim_ref, scan):
    t = pl.program_id(1)

    @pl.when(t == 0)
    def _():
        st_ref[:, 0:SSM_ST] = h0re_ref[...]
        st_ref[:, SSM_ST:2 * SSM_ST] = h0im_ref[...]

    bu_ref[...] = _dotb(u, bw_ref[...])
    scan(bu_ref, st_ref, are_ref, aim_ref)
    y = _dot(bu_ref[...].astype(BF16), cw_ref[...].astype(BF16)) + dsk_ref[...] * u

    @pl.when(t == pl.num_programs(1) - 1)
    def _():
        sre_ref[...] = st_ref[:, 0:SSM_ST]
        sim_ref[...] = st_ref[:, SSM_ST:2 * SSM_ST]

    return _gelu_tanh(y)


S5_NSUB = 4


def _s5_scan_pairs_sub(bu_ref, h_ref, ar, ai, carry):
    n_lc = len(ar)
    first_half = lax.broadcasted_iota(I32, (8, S5_LC), 0) < 4
    for j in range(bu_ref.shape[0] // 8):
        rows = slice(j * 8, j * 8 + 8)
        new = []
        for c in range(n_lc):
            cre = slice(c * S5_LC, (c + 1) * S5_LC)
            cim = slice(SSM_ST + c * S5_LC, SSM_ST + (c + 1) * S5_LC)
            hr, hi = carry[c]
            br = bu_ref[rows, cre]
            bi = bu_ref[rows, cim]
            er = ar[c] * hr - ai[c] * hi + br
            ei = ar[c] * hi + ai[c] * hr + bi
            sr = pltpu.roll(er, 4, 0)
            si = pltpu.roll(ei, 4, 0)
            orr = ar[c] * sr - ai[c] * si + br
            oi = ar[c] * si + ai[c] * sr + bi
            h_ref[rows, cre] = jnp.where(first_half, er, orr)
            h_ref[rows, cim] = jnp.where(first_half, ei, oi)
            new.append((pltpu.roll(orr, 4, 0), pltpu.roll(oi, 4, 0)))
        carry = new
    return carry


def _s5_prompt_kernel(*refs, nbatch, tc):
    u_refs = refs[:nbatch]
    (h0re_ref, h0im_ref, are_ref, aim_ref, bw_ref, cw_ref, dsk_ref,
     p_ref, sre_ref, sim_ref, r_ref, st_ref, y_ref) = refs[nbatch:nbatch + 13]
    bu_refs = refs[nbatch + 13:nbatch + 13 + S5_NSUB]
    h_refs = refs[nbatch + 13 + S5_NSUB:]
    t = pl.program_id(1)

    @pl.when(t == 0)
    def _():
        st_ref[:, 0:SSM_ST] = h0re_ref[...]
        st_ref[:, SSM_ST:2 * SSM_ST] = h0im_ref[...]

    nslab = r_ref.shape[0]
    for b in range(nbatch):
        ub = u_refs[b][...]
        for s in range(nslab):
            r_ref[s, pl.ds(b, tc, stride=nbatch), :] = ub[:, s * 128:(s + 1) * 128]
    u = jnp.concatenate([r_ref[s] for s in range(nslab)], axis=1)
    u_bf = u.astype(BF16)
    bw = bw_ref[...].astype(BF16)
    cw = cw_ref[...].astype(BF16)
    sub = u.shape[0] // S5_NSUB
    n_lc = SSM_ST // S5_LC
    ar = [are_ref[:, c * S5_LC:(c + 1) * S5_LC] for c in range(n_lc)]
    ai = [aim_ref[:, c * S5_LC:(c + 1) * S5_LC] for c in range(n_lc)]
    carry = [(st_ref[:, c * S5_LC:(c + 1) * S5_LC], st_ref[:, SSM_ST + c * S5_LC:SSM_ST + (c + 1) * S5_LC])
             for c in range(n_lc)]

    def b_proj(s):
        bu_refs[s][...] = _dot(u_bf[s * sub:(s + 1) * sub], bw)

    def c_proj(s):
        return _dot(h_refs[s][...].astype(BF16), cw)

    ys = []
    b_proj(0)
    for s in range(S5_NSUB):
        if s + 1 < S5_NSUB:
            b_proj(s + 1)
        carry = _s5_scan_pairs_sub(bu_refs[s], h_refs[s], ar, ai, carry)
        if s >= 1:
            ys.append(c_proj(s - 1))
    ys.append(c_proj(S5_NSUB - 1))
    for c in range(n_lc):
        st_ref[:, c * S5_LC:(c + 1) * S5_LC] = carry[c][0]
        st_ref[:, SSM_ST + c * S5_LC:SSM_ST + (c + 1) * S5_LC] = carry[c][1]
    y = _gelu_tanh(jnp.concatenate(ys, axis=0) + dsk_ref[...] * u)
    for s in range(nslab):
        y_ref[s] = y[:, s * 128:(s + 1) * 128]
    for b in range(nbatch):
        p_ref[b] = jnp.concatenate([y_ref[s, pl.ds(b, tc, stride=nbatch), :] for s in range(nslab)],
                                   axis=1).astype(p_ref.dtype)

    @pl.when(t == pl.num_programs(1) - 1)
    def _():
        sre_ref[...] = st_ref[:, 0:SSM_ST]
        sim_ref[...] = st_ref[:, SSM_ST:2 * SSM_ST]


def _s5_sample_kernel(u_ref, h0re_ref, h0im_ref, are_ref, aim_ref, bw_ref, cw_ref, dsk_ref,
                      p_ref, sre_ref, sim_ref, bu_ref, st_ref, *, nb, tc):
    scan = functools.partial(_s5_scan, nb=nb, tc=tc)
    p_ref[...] = _s5_core(u_ref[...], bu_ref, st_ref, h0re_ref, h0im_ref, are_ref, aim_ref, bw_ref, cw_ref,
                          dsk_ref, sre_ref, sim_ref, scan).astype(p_ref.dtype)


def _s5_params(a_re, a_im, log_dt, b_re, b_im, c_re, c_im):
    g, p = a_re.shape
    dt = jnp.exp(log_dt)[:, None]
    mag = jnp.exp(dt * a_re)
    ang = dt * a_im
    ab_re = mag * jnp.cos(ang)
    ab_im = mag * jnp.sin(ang)
    n_re = ab_re - 1.0
    n_im = ab_im
    den = a_re * a_re + a_im * a_im
    f_re = ((n_re * a_re + n_im * a_im) / den)[..., None]
    f_im = ((n_im * a_re - n_re * a_im) / den)[..., None]
    bb_re = f_re * b_re - f_im * b_im
    bb_im = f_re * b_im + f_im * b_re
    ngb = g // SSM_GB
    eye = jnp.eye(SSM_GB, dtype=F32)

    def bdiag_b(bb):
        t = bb.reshape(ngb, SSM_GB, p, SSM_GROUP).transpose(0, 1, 3, 2)
        t = t[:, :, :, None, :] * eye[None, :, None, :, None]
        return t.reshape(ngb, SSM_CH, SSM_ST)

    def bdiag_c(cc):
        t = cc.reshape(ngb, SSM_GB, SSM_GROUP, p).transpose(0, 1, 3, 2)
        t = t[:, :, :, None, :] * eye[None, :, None, :, None]
        return t.reshape(ngb, SSM_ST, SSM_CH)

    bw = jnp.concatenate([bdiag_b(bb_re), bdiag_b(bb_im)], axis=2)
    cw = jnp.concatenate([bdiag_c(c_re), -bdiag_c(c_im)], axis=1)
    are = jnp.broadcast_to(ab_re.reshape(1, g * p), (8, g * p))
    aim = jnp.broadcast_to(ab_im.reshape(1, g * p), (8, g * p))
    return are, aim, bw, cw


def _s5_common_specs(nb):
    return [
        pl.BlockSpec((nb, SSM_ST), lambda gb, t: (0, gb)),
        pl.BlockSpec((nb, SSM_ST), lambda gb, t: (0, gb)),
        pl.BlockSpec((8, SSM_ST), lambda gb, t: (0, gb)),
        pl.BlockSpec((8, SSM_ST), lambda gb, t: (0, gb)),
        pl.BlockSpec((None, SSM_CH, 2 * SSM_ST), lambda gb, t: (gb, 0, 0)),
        pl.BlockSpec((None, 2 * SSM_ST, SSM_CH), lambda gb, t: (gb, 0, 0)),
        pl.BlockSpec((1, SSM_CH), lambda gb, t: (0, gb)),
    ]


def _s5_prompt(h, nbatch, lp, params, dsk):
    assert nbatch == 4, "the paired scan packs two steps of 4 sequences into one 8-row tile"
    are, aim, bw, cw = params
    d = h.shape[1]
    ngb = d // SSM_CH
    tc = 256
    nt = lp // tc
    gp = are.shape[1]
    zeros = jnp.zeros((8, gp), F32)
    u_specs = [pl.BlockSpec((tc, SSM_CH), functools.partial(lambda gb, t, b: (b * nt + t, gb), b=b))
               for b in range(nbatch)]
    rows = tc * nbatch
    return pl.pallas_call(
        functools.partial(_s5_prompt_kernel, nbatch=nbatch, tc=tc),
        out_shape=(jax.ShapeDtypeStruct((nbatch, lp, d), BF16),
                   jax.ShapeDtypeStruct((8, gp), F32), jax.ShapeDtypeStruct((8, gp), F32)),
        grid=(ngb, nt),
        in_specs=u_specs + _s5_common_specs(8),
        out_specs=(pl.BlockSpec((nbatch, tc, SSM_CH), lambda gb, t: (0, t, gb)),
                   pl.BlockSpec((8, SSM_ST), lambda gb, t: (0, gb)),
                   pl.BlockSpec((8, SSM_ST), lambda gb, t: (0, gb))),
        scratch_shapes=[pltpu.VMEM((SSM_CH // 128, rows, 128), F32), pltpu.VMEM((8, 2 * SSM_ST), F32),
                        pltpu.VMEM((SSM_CH // 128, rows, 128), F32)]
        + [pltpu.VMEM((rows // S5_NSUB, 2 * SSM_ST), F32)] * (2 * S5_NSUB),
        compiler_params=_cparams(("parallel", "arbitrary")),
        name="s5_prompt",
    )(*([h] * nbatch), zeros, zeros, are, aim, bw, cw, dsk)


def _s5_sample(h, mp, bs, ls, h0re, h0im, params, dsk):
    are, aim, bw, cw = params
    d = h.shape[1]
    ngb = d // SSM_CH
    rows = ls * bs
    gp = are.shape[1]
    blk0 = mp // rows
    return pl.pallas_call(
        functools.partial(_s5_sample_kernel, nb=bs, tc=ls),
        out_shape=(jax.ShapeDtypeStruct((rows, d), BF16),
                   jax.ShapeDtypeStruct((bs, gp), F32), jax.ShapeDtypeStruct((bs, gp), F32)),
        grid=(ngb, 1),
        in_specs=[pl.BlockSpec((rows, SSM_CH), lambda gb, t: (blk0, gb))] + _s5_common_specs(bs),
        out_specs=(pl.BlockSpec((rows, SSM_CH), lambda gb, t: (0, gb)),
                   pl.BlockSpec((bs, SSM_ST), lambda gb, t: (0, gb)),
                   pl.BlockSpec((bs, SSM_ST), lambda gb, t: (0, gb))),
        scratch_shapes=[pltpu.VMEM((rows, 2 * SSM_ST), F32), pltpu.VMEM((bs, 2 * SSM_ST), F32)],
        compiler_params=_cparams(("parallel", "arbitrary")),
        name="s5_sample",
    )(h, h0re, h0im, are, aim, bw, cw, dsk)


def _ln_silu(y, g, b):
    mu = jnp.mean(y, axis=-1, keepdims=True)
    yc = y - mu
    z = yc * lax.rsqrt(jnp.mean(yc * yc, axis=-1, keepdims=True) + EPS) * g + b
    return _silu(z)


def _conv_prompt_kernel(cur_ref, halo_ref, buf_ref, w_ref, *rest, width, mode):
    if mode == "cf":
        bdw_ref, lng_ref, lnb_ref, o_ref, s_ref, acc_ref = rest
    else:
        bg_ref, o_ref, s_ref, acc_ref = rest
    t = pl.program_id(1)
    hb = halo_ref.shape[0]
    tc, d = cur_ref.shape
    first = t == 0
    s_ref[0:hb, :] = jnp.where(first, buf_ref[0], halo_ref[...])
    s_ref[hb:hb + tc, :] = cur_ref[...]
    s_ref[hb + tc:hb + tc + 8, :] = jnp.zeros((8, d), F32)
    off = hb - (width - 1)
    by_shift = [[(o // 8, o - off) for o in range(off, off + width) if o % 8 == sh] for sh in range(8)]
    rc, lch = 64, 256
    for r0 in range(0, tc, rc):
        for c in range(d // lch):
            cols = pl.ds(c * lch, lch)
            out = None
            for sh, taps in enumerate(by_shift):
                z = None
                for q, k in taps:
                    t = w_ref[k:k + 1, cols] * s_ref[r0 + 8 * q:r0 + 8 * q + rc + 8, cols]
                    z = t if z is None else z + t
                if z is not None:
                    zs = z[sh:sh + rc]
                    out = zs if out is None else out + zs
            acc_ref[r0:r0 + rc, cols] = out
    if mode == "cf":
        o_ref[...] = _ln_silu(acc_ref[...] + bdw_ref[...], lng_ref[...], lnb_ref[...]).astype(o_ref.dtype)
    else:
        o_ref[...] = (bg_ref[...] * acc_ref[...]).astype(o_ref.dtype)


def _conv_prompt(v, nbatch, lp, w, layer, mode, extra):
    d = v.shape[1]
    width = w.shape[1]
    hb = 32 if width > 9 else 8
    tc = 256
    nt = lp // tc
    zeros = jnp.zeros((nbatch, hb, d), F32)
    in_specs = [
        pl.BlockSpec((tc, d), lambda b, t: (b * nt + t, 0)),
        pl.BlockSpec((hb, d), lambda b, t: (jnp.maximum((b * lp + t * tc) // hb - 1, 0), 0)),
        pl.BlockSpec((1, hb, d), lambda b, t: (b, 0, 0)),
        pl.BlockSpec((None, width, d), lambda b, t: (layer, 0, 0)),
    ]
    if mode == "cf":
        in_specs += [pl.BlockSpec((1, d), lambda b, t: (layer, 0))] * 3
    else:
        in_specs += [pl.BlockSpec((tc, d), lambda b, t: (b * nt + t, 0))]
    return pl.pallas_call(
        functools.partial(_conv_prompt_kernel, width=width, mode=mode),
        out_shape=jax.ShapeDtypeStruct((nbatch * lp, d), BF16),
        grid=(nbatch, nt),
        in_specs=in_specs,
        out_specs=pl.BlockSpec((tc, d), lambda b, t: (b * nt + t, 0)),
        scratch_shapes=[pltpu.VMEM((hb + tc + 8, d), F32), pltpu.VMEM((tc, d), F32)],
        compiler_params=_cparams(("parallel", "arbitrary")),
        name="conv_prompt_" + mode,
    )(v, v, zeros, w, *extra)


def _conv_sample_kernel(v_ref, cache_ref, w_ref, *rest, width, mode):
    if mode == "cf":
        bdw_ref, lng_ref, lnb_ref, o_ref = rest
    else:
        bg_ref, o_ref = rest
    ls = v_ref.shape[0]
    hist = width - 1

    def full(j):
        return cache_ref[j] if j < hist else v_ref[j - hist]

    for l in range(ls):
        acc = w_ref[0:1, :] * full(l)
        for k in range(1, width):
            acc = acc + w_ref[k:k + 1, :] * full(l + k)
        if mode == "cf":
            o_ref[l] = _ln_silu(acc + bdw_ref[...], lng_ref[...], lnb_ref[...]).astype(o_ref.dtype)
        else:
            o_ref[l] = (bg_ref[l] * acc).astype(o_ref.dtype)


def _conv_sample(v_tm, cache_tm, w, layer, mode, extra):
    ls, bs, d = v_tm.shape
    width = w.shape[1]
    bc = 16
    in_specs = [
        pl.BlockSpec((ls, bc, d), lambda i: (0, i, 0)),
        pl.BlockSpec((width - 1, bc, d), lambda i: (0, i, 0)),
        pl.BlockSpec((None, width, d), lambda i: (layer, 0, 0)),
    ]
    if mode == "cf":
        in_specs += [pl.BlockSpec((1, d), lambda i: (layer, 0))] * 3
    else:
        in_specs += [pl.BlockSpec((ls, bc, d), lambda i: (0, i, 0))]
    return pl.pallas_call(
        functools.partial(_conv_sample_kernel, width=width, mode=mode),
        out_shape=jax.ShapeDtypeStruct((ls, bs, d), BF16),
        grid=(bs // bc,),
        in_specs=in_specs,
        out_specs=pl.BlockSpec((ls, bc, d), lambda i: (0, i, 0)),
        compiler_params=_cparams(("parallel",)),
        name="conv_sample_" + mode,
    )(v_tm, cache_tm, w, *extra)


def _onehot2(r):
    lane = lax.broadcasted_iota(I32, r.shape, 1).astype(F32)
    return jnp.where((lane == r[:, 0:1]) | (lane == r[:, 1:2]), 1.0, 0.0)


def _moe_plan_kernel(r_ref, pos_ref, info_ref, cnt_ref, run_ref, pst_ref):
    ph = pl.program_id(0)
    i = pl.program_id(1)
    r = r_ref[...]
    tb = r.shape[0]
    oh = _onehot2(r)
    colsum = jnp.broadcast_to(jnp.sum(oh, axis=0, keepdims=True), (8, ROUTE_LANES))

    @pl.when((ph == 0) & (i == 0))
    def _():
        cnt_ref[...] = jnp.zeros_like(cnt_ref)

    @pl.when(ph == 0)
    def _():
        cnt_ref[...] += colsum

    @pl.when((ph == 1) & (i == 0))
    def _():
        cnt = cnt_ref[...]
        nblk = jnp.floor((cnt + (MOE_BM - 1)) * (1.0 / MOE_BM))
        li = lax.broadcasted_iota(I32, (ROUTE_LANES, ROUTE_LANES), 0)
        lj = lax.broadcasted_iota(I32, (ROUTE_LANES, ROUTE_LANES), 1)
        upper = jnp.where(li <= lj, 1.0, 0.0).astype(BF16)
        bend = _dot(nblk.astype(BF16), upper)
        pst_ref[...] = (bend - nblk) * MOE_BM
        run_ref[...] = jnp.zeros_like(run_ref)
        n_used = bend[0:1, N_EXPERTS - 1:N_EXPERTS]
        bi = lax.broadcasted_iota(I32, (PLAN_ROWS, ROUTE_LANES), 0).astype(F32)
        ln = lax.broadcasted_iota(I32, (PLAN_ROWS, ROUTE_LANES), 1)
        ln_f = ln.astype(F32)
        pe = jnp.broadcast_to(bend[0:1, :], (PLAN_ROWS, ROUTE_LANES))
        pk = jnp.broadcast_to(nblk[0:1, :], (PLAN_ROWS, ROUTE_LANES))
        bcl = jnp.minimum(bi, n_used - 1.0)
        blk_e = jnp.sum(jnp.where((ln < N_EXPERTS) & (pe <= bcl), 1.0, 0.0), axis=1, keepdims=True)
        bend_col = jnp.sum(jnp.where(ln_f == bi, pe, 0.0), axis=1, keepdims=True)
        nblk_col = jnp.sum(jnp.where(ln_f == bi, pk, 0.0), axis=1, keepdims=True)
        info = jnp.where(ln == 0, blk_e, 0.0)
        info = jnp.where(ln == 1, n_used, info)
        info = jnp.where(ln == 2, bend_col, info)
        info = jnp.where(ln == 3, nblk_col, info)
        nonempty = (ln < N_EXPERTS) & (pk > 0.0)
        seq = jnp.sum(jnp.where(nonempty & (ln_f < blk_e), 1.0, 0.0), axis=1, keepdims=True)
        nxt = jnp.min(jnp.where(nonempty & (ln_f > blk_e), ln_f, float(ROUTE_LANES)), axis=1, keepdims=True)
        info = jnp.where(ln == 4, seq, info)
        info = jnp.where(ln == 5, nxt, info)
        info_ref[...] = info.astype(I32)

    @pl.when(ph == 1)
    def _():
        ti = lax.broadcasted_iota(I32, (tb, tb), 0)
        tj = lax.broadcasted_iota(I32, (tb, tb), 1)
        lower = jnp.where(ti > tj, 1.0, 0.0).astype(BF16)
        before = _dot(lower, oh.astype(BF16))
        val = pst_ref[0:1, :] + run_ref[0:1, :] + before
        lane = lax.broadcasted_iota(I32, r.shape, 1)
        lane_f = lane.astype(F32)
        p0 = jnp.sum(jnp.where(lane_f == r[:, 0:1], val, 0.0), axis=1, keepdims=True)
        p1 = jnp.sum(jnp.where(lane_f == r[:, 1:2], val, 0.0), axis=1, keepdims=True)
        pos_ref[...] = jnp.where(lane == 0, p0, jnp.where(lane == 1, p1, 0.0)).astype(I32)
        run_ref[...] += colsum


def _moe_plan(route):
    m = route.shape[0]
    nb = m // PLAN_TB
    return pl.pallas_call(
        _moe_plan_kernel,
        out_shape=(jax.ShapeDtypeStruct((m, ROUTE_LANES), I32), jax.ShapeDtypeStruct((PLAN_ROWS, ROUTE_LANES), I32)),
        grid=(2, nb),
        in_specs=[pl.BlockSpec((PLAN_TB, ROUTE_LANES), lambda ph, i: (i, 0))],
        out_specs=(pl.BlockSpec((PLAN_TB, ROUTE_LANES), lambda ph, i: (ph * i, 0)),
                   pl.BlockSpec((PLAN_ROWS, ROUTE_LANES), lambda ph, i: (0, 0))),
        scratch_shapes=[pltpu.VMEM((8, ROUTE_LANES), F32)] * 3,
        compiler_params=_cparams(("arbitrary", "arbitrary")),
        name="moe_plan",
    )(route)


def _moe_invert_kernel(pos_ref, tok_ref):
    def init(r, c):
        tok_ref[r] = 0
        return c

    def put(a, c):
        tok_ref[pos_ref[a]] = a >> 1
        return c

    lax.fori_loop(0, tok_ref.shape[0], init, 0, unroll=8)
    lax.fori_loop(0, pos_ref.shape[0], put, 0, unroll=8)


def _moe_invert(pos2, n_rows):
    grid_spec = pltpu.PrefetchScalarGridSpec(
        num_scalar_prefetch=1,
        grid=(1,),
        in_specs=[],
        out_specs=pl.BlockSpec(memory_space=pltpu.SMEM),
    )
    return pl.pallas_call(
        _moe_invert_kernel,
        out_shape=jax.ShapeDtypeStruct((n_rows,), I32),
        grid_spec=grid_spec,
        compiler_params=_cparams(("arbitrary",)),
        name="moe_invert",
    )(pos2)


def _moe_expert_kernel(be_ref, nu_ref, seq_ref, nxt_ref, tok_ref, h_hbm, w13_hbm, w2_hbm, y_ref,
                       xbuf, w13f_ref, w2f_ref, w13b_ref, w2b_ref, gsem, sem13, sem2, *, layer):
    i = pl.program_id(0)
    n_used = nu_ref[0]
    used = i < n_used
    e = be_ref[i]
    new_expert = (i == 0) | (e != be_ref[jnp.maximum(i - 1, 0)])
    wslot = seq_ref[i] % 2
    xslot = i % 2

    def gather(blk, s):
        def row(r, c):
            tok = tok_ref[blk * MOE_BM + r]
            pltpu.make_async_copy(h_hbm.at[pl.ds(tok, 1)], xbuf.at[s, pl.ds(r, 1)], gsem.at[s]).start()
            return c
        lax.fori_loop(0, MOE_BM, row, 0, unroll=8)

    def fetch(expert, s):
        return (pltpu.make_async_copy(w13_hbm.at[layer, expert], w13f_ref.at[s], sem13.at[s]),
                pltpu.make_async_copy(w2_hbm.at[layer, expert], w2f_ref.at[s], sem2.at[s]))

    @pl.when(i == 0)
    def _():
        gather(0, 0)
        for c in fetch(e, wslot):
            c.start()

    @pl.when(i + 1 < n_used)
    def _():
        gather(i + 1, 1 - xslot)

    @pl.when(used & new_expert)
    def _():
        nxt = nxt_ref[i]

        @pl.when(nxt < N_EXPERTS)
        def _():
            for c in fetch(nxt, 1 - wslot):
                c.start()

        for c in fetch(e, wslot):
            c.wait()
        w13b_ref[...] = w13f_ref[wslot].astype(BF16)
        w2b_ref[...] = w2f_ref[wslot].astype(BF16)

    @pl.when(used)
    def _():
        pltpu.make_async_copy(h_hbm.at[pl.ds(0, MOE_BM)], xbuf.at[xslot], gsem.at[xslot]).wait()
        ab = _dot(xbuf[xslot].astype(BF16), w13b_ref[...])
        a = ab[:, 0:D_EXPERT]
        b = ab[:, D_EXPERT:2 * D_EXPERT]
        y_ref[...] = _dot((_silu(a) * b).astype(BF16), w2b_ref[...])

    @pl.when(jnp.logical_not(used))
    def _():
        y_ref[...] = jnp.zeros_like(y_ref)


def _moe_experts(h, blk_e, n_used, seq, nxt, row_tok, w13, w2, layer):
    d = h.shape[1]
    n_blocks = blk_e.shape[0]
    de2 = w13.shape[3]
    grid_spec = pltpu.PrefetchScalarGridSpec(
        num_scalar_prefetch=5,
        grid=(n_blocks,),
        in_specs=[pl.BlockSpec(memory_space=pl.ANY)] * 3,
        out_specs=pl.BlockSpec((MOE_BM, d), lambda i, *_: (i, 0)),
        scratch_shapes=[pltpu.VMEM((2, MOE_BM, d), F32),
                        pltpu.VMEM((2, d, de2), F32), pltpu.VMEM((2, de2 // 2, d), F32),
                        pltpu.VMEM((d, de2), BF16), pltpu.VMEM((de2 // 2, d), BF16),
                        pltpu.SemaphoreType.DMA((2,)), pltpu.SemaphoreType.DMA((2,)),
                        pltpu.SemaphoreType.DMA((2,))],
    )
    return pl.pallas_call(
        functools.partial(_moe_expert_kernel, layer=layer),
        out_shape=jax.ShapeDtypeStruct((n_blocks * MOE_BM, d), F32),
        grid_spec=grid_spec,
        compiler_params=_cparams(("arbitrary",)),
        name="moe_experts",
    )(blk_e, n_used, seq, nxt, row_tok, h, w13, w2)


def _moe_combine_kernel(pos_ref, y_ref, x_ref, r_ref, gp_ref, gs_ref, o_ref, ybuf, sem, *, npb, bpb):
    i = pl.program_id(0)
    n = pl.num_programs(0)
    slot = i % 2

    def gather(blk, s):
        def row(r, c):
            base = (blk * TMC + r) * 2
            for k in range(2):
                p = pos_ref[base + k]
                pltpu.make_async_copy(y_ref.at[pl.ds(p, 1)], ybuf.at[s, k, pl.ds(r, 1)], sem.at[s]).start()
            return c
        lax.fori_loop(0, TMC, row, 0, unroll=4)

    @pl.when(i == 0)
    def _():
        gather(0, 0)

    @pl.when(i + 1 < n)
    def _():
        gather(i + 1, 1 - slot)

    for k in range(2):
        pltpu.make_async_copy(y_ref.at[pl.ds(0, TMC)], ybuf.at[slot, k], sem.at[slot]).wait()

    r = r_ref[...]
    f = r[:, 2:3] * ybuf[slot, 0] + r[:, 3:4] * ybuf[slot, 1]
    gate = _seq_vec(gp_ref, gs_ref, i, npb, bpb, TMC)
    o_ref[...] = x_ref[...] + gate * f


def _moe_combine(y, pos2, x, route, ap, as_, gate_idx, mp, lp):
    m, d = x.shape
    bs = as_.shape[0]
    grid_spec = pltpu.PrefetchScalarGridSpec(
        num_scalar_prefetch=1,
        grid=(m // TMC,),
        in_specs=[
            pl.BlockSpec(memory_space=pl.ANY),
            pl.BlockSpec((TMC, d), lambda i, pos: (i, 0)),
            pl.BlockSpec((TMC, ROUTE_LANES), lambda i, pos: (i, 0)),
            pl.BlockSpec((8, d), lambda i, pos: (0, gate_idx)),
            pl.BlockSpec((bs, d), lambda i, pos: (0, gate_idx)),
        ],
        out_specs=pl.BlockSpec((TMC, d), lambda i, pos: (i, 0)),
        scratch_shapes=[pltpu.VMEM((2, 2, TMC, d), F32), pltpu.SemaphoreType.DMA((2,))],
    )
    return pl.pallas_call(
        functools.partial(_moe_combine_kernel, npb=mp // TMC, bpb=lp // TMC),
        out_shape=jax.ShapeDtypeStruct((m, d), F32),
        grid_spec=grid_spec,
        compiler_params=_cparams(("arbitrary",)),
        name="moe_combine",
    )(pos2, y, x, route, ap, as_)


def _moe_layer(x, h2, route, w13, w2, layer, ap, as_, mp, lp):
    m = x.shape[0]
    n_blocks = (2 * m + N_EXPERTS * (MOE_BM - 1) + MOE_BM - 1) // MOE_BM
    assert n_blocks <= PLAN_ROWS and m % PLAN_TB == 0 and m % TMC == 0
    pos, info = _moe_plan(route)
    pos2 = pos[:, 0:2].reshape(2 * m)
    blk_e, n_used = info[:n_blocks, 0], info[0:1, 1]
    seq, nxt = info[:n_blocks, 4], info[:n_blocks, 5]
    row_tok = _moe_invert(pos2, n_blocks * MOE_BM)
    y = _moe_experts(h2, blk_e, n_used, seq, nxt, row_tok, w13, w2, layer)
    return _moe_combine(y, pos2, x, route, ap, as_, 5, mp, lp)


def kernel(x_prompt, x_sample, c_prompt, c_sample, state_l0_ssm_re, state_l0_ssm_im, cache_l1_conformer_conv, cache_l2_short_conv, state_l3_ssm_re, state_l3_ssm_im, norm1_g, norm2_g, w_ada, b_ada, final_norm_g, ssm_a_re, ssm_a_im, ssm_log_dt, ssm_b_re, ssm_b_im, ssm_c_re, ssm_c_im, ssm_d, ssm_w_glu, cf_w_pw1, cf_w_dw, cf_b_dw, cf_ln_g, cf_ln_b, cf_w_pw2, sc_w_in, sc_w_conv, sc_w_out, moe_w_group, moe_b_group, moe_w_expert, moe_b_expert, moe_w13, moe_w2):
    bp, lp, d = x_prompt.shape
    bs, ls, _ = x_sample.shape
    depth = w_ada.shape[0]
    mp, ms = bp * lp, bs * ls
    g_ssm, p_ssm = ssm_a_re.shape[1:]
    assert d == D_MODEL and lp % TM == 0 and ms % TM == 0 and TM % bs == 0 and bp <= 8 and bs % 16 == 0
    assert TMC % bs == 0 and mp % ms == 0

    x = jnp.concatenate([x_prompt.reshape(mp, d), x_sample.transpose(1, 0, 2).reshape(ms, d)], axis=0)
    c_all = jnp.concatenate([c_prompt, jnp.zeros((8 - bp, d), F32), c_sample], axis=0)
    ada = _ada_all(c_all, w_ada, b_ada)
    n_pad = ROUTE_LANES - N_GROUPS - N_EXPERTS
    wr_all = jnp.concatenate([moe_w_group, moe_w_expert, jnp.zeros((depth, d, n_pad), F32)], axis=2)
    br_all = jnp.concatenate([moe_b_group, moe_b_expert, jnp.zeros((depth, n_pad), F32)], axis=1)

    sample_init = [(state_l0_ssm_re, state_l0_ssm_im), (cache_l1_conformer_conv,), (cache_l2_short_conv,),
                   (state_l3_ssm_re, state_l3_ssm_im)]
    p_states, s_states = [], []

    for i in range(depth):
        ap, as_ = ada[i, 0:8], ada[i, 8:]
        kind, j = i % 3, i // 3
        if kind == 0:
            h = _modulate(x, norm1_g[i:i + 1], ap, as_, 0, 1, mp, lp, F32)
            params = _s5_params(ssm_a_re[j], ssm_a_im[j], ssm_log_dt[j], ssm_b_re[j], ssm_b_im[j],
                                ssm_c_re[j], ssm_c_im[j])
            dsk = ssm_d[j:j + 1]
            pp, pre, pim = _s5_prompt(h, bp, lp, params, dsk)
            h0re, h0im = sample_init[i]
            psm, sre, sim = _s5_sample(h, mp, bs, ls, h0re.reshape(bs, g_ssm * p_ssm),
                                       h0im.reshape(bs, g_ssm * p_ssm), params, dsk)
            p_states.append((pre[:bp].reshape(bp, g_ssm, p_ssm), pim[:bp].reshape(bp, g_ssm, p_ssm)))
            s_states.append((sre.reshape(bs, g_ssm, p_ssm), sim.reshape(bs, g_ssm, p_ssm)))
            pre_out = (pp.reshape(mp, d), psm)
            w_out = ssm_w_glu
        elif kind == 1:
            h = _modulate(x, norm1_g[i:i + 1], ap, as_, 0, 1, mp, lp, BF16)
            gl = _mm_split((h,), cf_w_pw1, j, 2, _mm_glu_kernel, 1, 512, mp, "mm_glu")
            extra = (cf_b_dw, cf_ln_g, cf_ln_b)
            pp = _conv_prompt(gl, bp, lp, cf_w_dw, j, "cf", extra)
            (cache,) = sample_init[i]
            g_tm = gl[mp:].reshape(ls, bs, d)
            psm = _conv_sample(g_tm, cache.transpose(1, 0, 2), cf_w_dw, j, "cf", extra)
            hist = CF_WIDTH - 1
            p_states.append((jnp.stack([gl[b * lp + lp - hist:(b + 1) * lp] for b in range(bp)]),))
            s_states.append((jnp.concatenate([cache, g_tm.transpose(1, 0, 2)], axis=1)[:, -hist:],))
            pre_out = (pp, psm.reshape(ms, d))
            w_out = cf_w_pw2
        else:
            h = _modulate(x, norm1_g[i:i + 1], ap, as_, 0, 1, mp, lp, BF16)
            bg, v = _mm_split((h,), sc_w_in, j, 3, _mm_bcv_kernel, 2, 512, mp, "mm_bcv")
            pp = _conv_prompt(v, bp, lp, sc_w_conv, j, "sc", (bg,))
            (cache,) = sample_init[i]
            v_tm = v[mp:].reshape(ls, bs, d)
            psm = _conv_sample(v_tm, cache.transpose(1, 0, 2), sc_w_conv, j, "sc", (bg[mp:].reshape(ls, bs, d),))
            hist = SC_WIDTH - 1
            p_states.append((jnp.stack([v[b * lp + lp - hist:(b + 1) * lp] for b in range(bp)]),))
            s_states.append((jnp.concatenate([cache, v_tm.transpose(1, 0, 2)], axis=1)[:, -hist:],))
            pre_out = (pp, psm.reshape(ms, d))
            w_out = sc_w_out

        x = _mm_resid(pre_out, w_out, j, x, ap, as_, 2, mp, lp, glu=(kind == 0))

        h2, route = _modulate(x, norm2_g[i:i + 1], ap, as_, 3, 4, mp, lp, F32, router=(wr_all[i], br_all[i:i + 1]))
        x = _moe_layer(x, h2, route, moe_w13, moe_w2, i, ap, as_, mp, lp)

    y_p, y_s = _final_norm(x, final_norm_g.reshape(1, d), mp)
    y_prompt = y_p.reshape(bp, lp, d)
    y_sample = y_s.reshape(ls, bs, d).transpose(1, 0, 2)
    ps, ss = p_states, s_states
    return (y_prompt, y_sample,
            ps[0][0], ps[0][1], ps[1][0], ps[2][0], ps[3][0], ps[3][1],
            ss[0][0], ss[0][1], ss[1][0], ss[2][0], ss[3][0], ss[3][1])
```

```python
import functools

import jax
import jax.numpy as jnp
from jax import lax
from jax.experimental import pallas as pl
from jax.experimental.pallas import tpu as pltpu

F32 = jnp.float32
BF16 = jnp.bfloat16
I32 = jnp.int32

D_MODEL = 2048
SSM_GROUP = 16
SSM_STATE = 64
SSM_GB = 16
SSM_CH = SSM_GB * SSM_GROUP
SSM_ST = SSM_GB * SSM_STATE
CF_WIDTH = 31
SC_WIDTH = 3
N_GROUPS = 8
EPG = 8
N_EXPERTS = N_GROUPS * EPG
D_EXPERT = D_MODEL // 4
EPS = 1e-6

TM = 512
MOE_BM = 128
TMC = 256
TMS = 512
PLAN_TB = 512
PLAN_ROWS = 256
ROUTE_LANES = 128
VMEM_LIMIT = 56 * 1024 * 1024


def _cparams(sem):
    return pltpu.CompilerParams(dimension_semantics=sem, vmem_limit_bytes=VMEM_LIMIT)


def _dot(a, b):
    return jnp.dot(a, b, preferred_element_type=F32)


def _dotb(a, b):
    return _dot(a.astype(BF16), b.astype(BF16))


def _sigmoid(x):
    return 1.0 / (1.0 + jnp.exp(-x))


def _silu(x):
    return x * _sigmoid(x)


def _gelu_tanh(x):
    return 0.5 * x * (1.0 + jnp.tanh(0.7978845608028654 * (x + 0.044715 * (x * x * x))))


def _seq_vec(vp_ref, vs_ref, blk, n_prompt_blocks, blocks_per_batch, rows):
    b = jnp.minimum(blk // blocks_per_batch, vp_ref.shape[0] - 1)
    vp = vp_ref[pl.ds(b, 1), :]
    vs = vs_ref[...]
    reps = rows // vs.shape[0]
    if reps > 1:
        vs = jnp.concatenate([vs] * reps, axis=0)
    return jnp.where(blk < n_prompt_blocks, vp, vs)


def _ada_kernel(c_ref, w_ref, b_ref, o_ref):
    c = c_ref[...]
    o_ref[0] = _dotb(_silu(c), w_ref[0]) + b_ref[0]


def _ada_all(c_all, w_ada, b_ada):
    depth, d, n = w_ada.shape
    rows = c_all.shape[0]
    tn = 1024
    return pl.pallas_call(
        _ada_kernel,
        out_shape=jax.ShapeDtypeStruct((depth, rows, n), F32),
        grid=(depth, n // tn),
        in_specs=[
            pl.BlockSpec((rows, d), lambda l, j: (0, 0)),
            pl.BlockSpec((1, d, tn), lambda l, j: (l, 0, j)),
            pl.BlockSpec((1, 1, tn), lambda l, j: (l, 0, j)),
        ],
        out_specs=pl.BlockSpec((1, rows, tn), lambda l, j: (l, 0, j)),
        compiler_params=_cparams(("parallel", "parallel")),
        name="ada_proj",
    )(c_all, w_ada, b_ada.reshape(depth, 1, n))


def _rms(x, g):
    return x * lax.rsqrt(jnp.mean(x * x, axis=-1, keepdims=True) + EPS) * g


def _modulate_kernel(x_ref, g_ref, shp_ref, scp_ref, shs_ref, scs_ref, o_ref, *, npb, bpb):
    i = pl.program_id(0)
    rows = x_ref.shape[0]
    y = _rms(x_ref[...], g_ref[...])
    sc = _seq_vec(scp_ref, scs_ref, i, npb, bpb, rows)
    sh = _seq_vec(shp_ref, shs_ref, i, npb, bpb, rows)
    o_ref[...] = (y * (1.0 + sc) + sh).astype(o_ref.dtype)


def _route_from_logits(lg):
    rows = lg.shape[0]
    lane = lax.broadcasted_iota(I32, (rows, ROUTE_LANES), 1)
    lane_f = lane.astype(F32)
    neg = jnp.float32(-jnp.inf)
    big = jnp.float32(ROUTE_LANES)
    is_g = lane < N_GROUPS
    gl = jnp.where(is_g, lg, neg)
    gmax = jnp.max(gl, axis=1, keepdims=True)
    grp = jnp.min(jnp.where(gl == gmax, lane_f, big), axis=1, keepdims=True)
    p_grp = 1.0 / jnp.sum(jnp.where(is_g, jnp.exp(gl - gmax), 0.0), axis=1, keepdims=True)
    e_grp = ((lane - N_GROUPS) >> 3).astype(F32)
    valid = (lane >= N_GROUPS) & (lane < N_GROUPS + N_EXPERTS) & (e_grp == grp)
    el = jnp.where(valid, lg, neg)
    v1 = jnp.max(el, axis=1, keepdims=True)
    i1 = jnp.min(jnp.where(el == v1, lane_f, big), axis=1, keepdims=True)
    el2 = jnp.where(lane_f == i1, neg, el)
    v2 = jnp.max(el2, axis=1, keepdims=True)
    i2 = jnp.min(jnp.where(el2 == v2, lane_f, big), axis=1, keepdims=True)
    t = jnp.exp(v2 - v1)
    den = 1.0 + t
    w1 = (1.0 / den) * p_grp
    w2 = (t / den) * p_grp
    out = jnp.where(lane == 0, i1 - N_GROUPS, 0.0)
    out = jnp.where(lane == 1, i2 - N_GROUPS, out)
    out = jnp.where(lane == 2, w1, out)
    out = jnp.where(lane == 3, w2, out)
    return out


def _modulate_route_kernel(x_ref, g_ref, shp_ref, scp_ref, shs_ref, scs_ref, wr_ref, br_ref,
                           o_ref, r_ref, *, npb, bpb):
    i = pl.program_id(0)
    rows = x_ref.shape[0]
    y = _rms(x_ref[...], g_ref[...])
    sc = _seq_vec(scp_ref, scs_ref, i, npb, bpb, rows)
    sh = _seq_vec(shp_ref, shs_ref, i, npb, bpb, rows)
    h = y * (1.0 + sc) + sh
    o_ref[...] = h
    lg = _dotb(h, wr_ref[...]) + br_ref[...]
    r_ref[...] = _route_from_logits(lg)


def _modulate(x, g, ap, as_, sh_idx, sc_idx, mp, lp, out_dtype, router=None):
    m, d = x.shape
    bs = as_.shape[0]
    npb, bpb = mp // TM, lp // TM
    in_specs = [
        pl.BlockSpec((TM, d), lambda i: (i, 0)),
        pl.BlockSpec((1, d), lambda i: (0, 0)),
        pl.BlockSpec((8, d), lambda i: (0, sh_idx)),
        pl.BlockSpec((8, d), lambda i: (0, sc_idx)),
        pl.BlockSpec((bs, d), lambda i: (0, sh_idx)),
        pl.BlockSpec((bs, d), lambda i: (0, sc_idx)),
    ]
    if router is None:
        return pl.pallas_call(
            functools.partial(_modulate_kernel, npb=npb, bpb=bpb),
            out_shape=jax.ShapeDtypeStruct((m, d), out_dtype),
            grid=(m // TM,),
            in_specs=in_specs,
            out_specs=pl.BlockSpec((TM, d), lambda i: (i, 0)),
            compiler_params=_cparams(("parallel",)),
            name="modulate",
        )(x, g, ap, ap, as_, as_)
    wr, br = router
    return pl.pallas_call(
        functools.partial(_modulate_route_kernel, npb=npb, bpb=bpb),
        out_shape=(jax.ShapeDtypeStruct((m, d), F32), jax.ShapeDtypeStruct((m, ROUTE_LANES), F32)),
        grid=(m // TM,),
        in_specs=in_specs + [
            pl.BlockSpec((d, ROUTE_LANES), lambda i: (0, 0)),
            pl.BlockSpec((1, ROUTE_LANES), lambda i: (0, 0)),
        ],
        out_specs=(pl.BlockSpec((TM, d), lambda i: (i, 0)), pl.BlockSpec((TM, ROUTE_LANES), lambda i: (i, 0))),
        compiler_params=_cparams(("parallel",)),
        name="modulate_route",
    )(x, g, ap, ap, as_, as_, wr, br)


def _final_norm_kernel(x_ref, g_ref, op_ref, os_ref, *, npb):
    i = pl.program_id(0)
    y = _rms(x_ref[...], g_ref[...])

    @pl.when(i < npb)
    def _():
        op_ref[...] = y

    @pl.when(i >= npb)
    def _():
        os_ref[...] = y


def _final_norm(x, g, mp):
    m, d = x.shape
    npb = mp // TM
    return pl.pallas_call(
        functools.partial(_final_norm_kernel, npb=npb),
        out_shape=(jax.ShapeDtypeStruct((mp, d), F32), jax.ShapeDtypeStruct((m - mp, d), F32)),
        grid=(m // TM,),
        in_specs=[pl.BlockSpec((TM, d), lambda i: (i, 0)), pl.BlockSpec((1, d), lambda i: (0, 0))],
        out_specs=(pl.BlockSpec((TM, d), lambda i: (jnp.minimum(i, npb - 1), 0)),
                   pl.BlockSpec((TM, d), lambda i: (jnp.maximum(i - npb, 0), 0))),
        compiler_params=_cparams(("arbitrary",)),
        name="final_norm",
    )(x, g)


def _a_specs(a_parts, npb):
    k = a_parts[0].shape[1]
    if len(a_parts) == 1:
        return [pl.BlockSpec((TM, k), lambda j, i: (i, 0))]
    return [pl.BlockSpec((TM, k), lambda j, i: (jnp.minimum(i, npb - 1), 0)),
            pl.BlockSpec((TM, k), lambda j, i: (jnp.maximum(i - npb, 0), 0))]


def _mm_products(a_refs, w_refs, wbf_ref, npb):
    i = pl.program_id(1)

    @pl.when(i == 0)
    def _():
        for s, w_ref in enumerate(w_refs):
            wbf_ref[s] = w_ref[...].astype(BF16)

    if len(a_refs) == 1:
        a = a_refs[0][...]
    else:
        a = jnp.where(i < npb, a_refs[0][...], a_refs[1][...])
    a = a.astype(BF16)
    return [_dot(a, wbf_ref[s]) for s in range(len(w_refs))]


def _mm_glu_kernel(*refs, n_a, npb):
    a_refs, (wa_ref, wb_ref, o_ref, wbf_ref) = refs[:n_a], refs[n_a:]
    za, zb = _mm_products(a_refs, (wa_ref, wb_ref), wbf_ref, npb)
    o_ref[...] = za * _sigmoid(zb)


def _mm_bcv_kernel(*refs, n_a, npb):
    a_refs, (w0_ref, w1_ref, w2_ref, bg_ref, v_ref, wbf_ref) = refs[:n_a], refs[n_a:]
    bg, cg, hv = _mm_products(a_refs, (w0_ref, w1_ref, w2_ref), wbf_ref, npb)
    bg_ref[...] = bg
    v_ref[...] = cg * hv


def _mm_resid_kernel(*refs, n_a, npb, bpb, glu):
    a_refs, rest = refs[:n_a], refs[n_a:]
    n_w = 2 if glu else 1
    w_refs = rest[:n_w]
    x_ref, gp_ref, gs_ref, o_ref, wbf_ref = rest[n_w:]
    z = _mm_products(a_refs, w_refs, wbf_ref, npb)
    z = z[0] * _sigmoid(z[1]) if glu else z[0]
    gate = _seq_vec(gp_ref, gs_ref, pl.program_id(1), npb, bpb, x_ref.shape[0])
    o_ref[...] = x_ref[...] + gate * z


def _w_spec(k, tn, layer, col0):
    return pl.BlockSpec((None, k, tn), lambda j, i: (layer, 0, col0 + j))


def _mm_split(a_parts, w, layer, n_split, kernel_fn, n_out, tn, mp, name):
    m = sum(a.shape[0] for a in a_parts)
    k = a_parts[0].shape[1]
    n = w.shape[2] // n_split
    nj = n // tn
    out_shape = [jax.ShapeDtypeStruct((m, n), F32) for _ in range(n_out)]
    out_specs = [pl.BlockSpec((TM, tn), lambda j, i: (i, j)) for _ in range(n_out)]
    return pl.pallas_call(
        functools.partial(kernel_fn, n_a=len(a_parts), npb=mp // TM),
        out_shape=out_shape if n_out > 1 else out_shape[0],
        grid=(nj, m // TM),
        in_specs=_a_specs(a_parts, mp // TM) + [_w_spec(k, tn, layer, s * nj) for s in range(n_split)],
        out_specs=out_specs if n_out > 1 else out_specs[0],
        scratch_shapes=[pltpu.VMEM((n_split, k, tn), BF16)],
        compiler_params=_cparams(("parallel", "arbitrary")),
        name=name,
    )(*a_parts, *([w] * n_split))


def _mm_resid(a_parts, w, layer, x, ap, as_, gate_idx, mp, lp, glu=False):
    m = x.shape[0]
    k = a_parts[0].shape[1]
    n_split = 2 if glu else 1
    n = w.shape[2] // n_split
    tn = 512 if glu else 1024
    nj = n // tn
    bs = as_.shape[0]
    return pl.pallas_call(
        functools.partial(_mm_resid_kernel, n_a=len(a_parts), npb=mp // TM, bpb=lp // TM, glu=glu),
        out_shape=jax.ShapeDtypeStruct((m, n), F32),
        grid=(nj, m // TM),
        in_specs=_a_specs(a_parts, mp // TM)
        + [_w_spec(k, tn, layer, s * nj) for s in range(n_split)]
        + [
            pl.BlockSpec((TM, tn), lambda j, i: (i, j)),
            pl.BlockSpec((8, tn), lambda j, i: (0, gate_idx * nj + j)),
            pl.BlockSpec((bs, tn), lambda j, i: (0, gate_idx * nj + j)),
        ],
        out_specs=pl.BlockSpec((TM, tn), lambda j, i: (i, j)),
        scratch_shapes=[pltpu.VMEM((n_split, k, tn), BF16)],
        compiler_params=_cparams(("parallel", "arbitrary")),
        name="mm_glu_resid" if glu else "mm_resid",
    )(*a_parts, *([w] * n_split), x, ap, as_)


S5_LC = 512


def _s5_scan(bu_ref, st_ref, are_ref, aim_ref, nb, tc):
    n_tiles = nb // 8

    def tile(r, carry):
        r0 = pl.multiple_of(r * 8, 8)
        for c in range(SSM_ST // S5_LC):
            cre = pl.ds(c * S5_LC, S5_LC)
            cim = pl.ds(SSM_ST + c * S5_LC, S5_LC)
            ar = are_ref[:, cre]
            ai = aim_ref[:, cre]
            srow = pl.ds(r0, 8)
            hr = st_ref[srow, cre]
            hi = st_ref[srow, cim]

            def step(t, carry):
                hr, hi = carry
                row = pl.ds(pl.multiple_of(t * nb + r0, 8), 8)
                nr = ar * hr - ai * hi + bu_ref[row, cre]
                ni = ar * hi + ai * hr + bu_ref[row, cim]
                bu_ref[row, cre] = nr
                bu_ref[row, cim] = ni
                return nr, ni

            hr, hi = lax.fori_loop(0, tc, step, (hr, hi), unroll=8)
            st_ref[srow, cre] = hr
            st_ref[srow, cim] = hi
        return carry

    lax.fori_loop(0, n_tiles, tile, 0)


def _s5_core(u, bu_ref, st_ref, h0re_ref, h0im_ref, are_ref, aim_ref, bw_ref, cw_ref, dsk_ref,
             sre_ref, sim_ref, scan):
    t = pl.program_id(1)

    @pl.when(t == 0)
    def _():
        st_ref[:, 0:SSM_ST] = h0re_ref[...]
        st_ref[:, SSM_ST:2 * SSM_ST] = h0im_ref[...]

    bu_ref[...] = _dotb(u, bw_ref[...])
    scan(bu_ref, st_ref, are_ref, aim_ref)
    y = _dot(bu_ref[...].astype(BF16), cw_ref[...].astype(BF16)) + dsk_ref[...] * u

    @pl.when(t == pl.num_programs(1) - 1)
    def _():
        sre_ref[...] = st_ref[:, 0:SSM_ST]
        sim_ref[...] = st_ref[:, SSM_ST:2 * SSM_ST]

    return _gelu_tanh(y)


S5_NSUB = 4


def _s5_scan_pairs_sub(bu_ref, h_ref, ar, ai, carry):
    n_lc = len(ar)
    first_half = lax.broadcasted_iota(I32, (8, S5_LC), 0) < 4
    for j in range(bu_ref.shape[0] // 8):
        rows = slice(j * 8, j * 8 + 8)
        new = []
        for c in range(n_lc):
            cre = slice(c * S5_LC, (c + 1) * S5_LC)
            cim = slice(SSM_ST + c * S5_LC, SSM_ST + (c + 1) * S5_LC)
            hr, hi = carry[c]
            br = bu_ref[rows, cre]
            bi = bu_ref[rows, cim]
            er = ar[c] * hr - ai[c] * hi + br
            ei = ar[c] * hi + ai[c] * hr + bi
            sr = pltpu.roll(er, 4, 0)
            si = pltpu.roll(ei, 4, 0)
            orr = ar[c] * sr - ai[c] * si + br
            oi = ar[c] * si + ai[c] * sr + bi
            h_ref[rows, cre] = jnp.where(first_half, er, orr)
            h_ref[rows, cim] = jnp.where(first_half, ei, oi)
            new.append((pltpu.roll(orr, 4, 0), pltpu.roll(oi, 4, 0)))
        carry = new
    return carry


def _s5_prompt_kernel(*refs, nbatch, tc):
    u_refs = refs[:nbatch]
    (h0re_ref, h0im_ref, are_ref, aim_ref, bw_ref, cw_ref, dsk_ref,
     p_ref, sre_ref, sim_ref, r_ref, st_ref, y_ref) = refs[nbatch:nbatch + 13]
    bu_refs = refs[nbatch + 13:nbatch + 13 + S5_NSUB]
    h_refs = refs[nbatch + 13 + S5_NSUB:]
    t = pl.program_id(1)

    @pl.when(t == 0)
    def _():
        st_ref[:, 0:SSM_ST] = h0re_ref[...]
        st_ref[:, SSM_ST:2 * SSM_ST] = h0im_ref[...]

    nslab = r_ref.shape[0]
    for b in range(nbatch):
        ub = u_refs[b][...]
        for s in range(nslab):
            r_ref[s, pl.ds(b, tc, stride=nbatch), :] = ub[:, s * 128:(s + 1) * 128]
    u = jnp.concatenate([r_ref[s] for s in range(nslab)], axis=1)
    u_bf = u.astype(BF16)
    bw = bw_ref[...].astype(BF16)
    cw = cw_ref[...].astype(BF16)
    sub = u.shape[0] // S5_NSUB
    n_lc = SSM_ST // S5_LC
    ar = [are_ref[:, c * S5_LC:(c + 1) * S5_LC] for c in range(n_lc)]
    ai = [aim_ref[:, c * S5_LC:(c + 1) * S5_LC] for c in range(n_lc)]
    carry = [(st_ref[:, c * S5_LC:(c + 1) * S5_LC], st_ref[:, SSM_ST + c * S5_LC:SSM_ST + (c + 1) * S5_LC])
             for c in range(n_lc)]

    def b_proj(s):
        bu_refs[s][...] = _dot(u_bf[s * sub:(s + 1) * sub], bw)

    def c_proj(s):
        return _dot(h_refs[s][...].astype(BF16), cw)

    ys = []
    b_proj(0)
    for s in range(S5_NSUB):
        if s + 1 < S5_NSUB:
            b_proj(s + 1)
        carry = _s5_scan_pairs_sub(bu_refs[s], h_refs[s], ar, ai, carry)
        if s >= 1:
            ys.append(c_proj(s - 1))
    ys.append(c_proj(S5_NSUB - 1))
    for c in range(n_lc):
        st_ref[:, c * S5_LC:(c + 1) * S5_LC] = carry[c][0]
        st_ref[:, SSM_ST + c * S5_LC:SSM_ST + (c + 1) * S5_LC] = carry[c][1]
    y = _gelu_tanh(jnp.concatenate(ys, axis=0) + dsk_ref[...] * u)
    for s in range(nslab):
        y_ref[s] = y[:, s * 128:(s + 1) * 128]
    for b in range(nbatch):
        p_ref[b] = jnp.concatenate([y_ref[s, pl.ds(b, tc, stride=nbatch), :] for s in range(nslab)],
                                   axis=1).astype(p_ref.dtype)

    @pl.when(t == pl.num_programs(1) - 1)
    def _():
        sre_ref[...] = st_ref[:, 0:SSM_ST]
        sim_ref[...] = st_ref[:, SSM_ST:2 * SSM_ST]


def _s5_sample_kernel(u_ref, h0re_ref, h0im_ref, are_ref, aim_ref, bw_ref, cw_ref, dsk_ref,
                      p_ref, sre_ref, sim_ref, bu_ref, st_ref, *, nb, tc):
    scan = functools.partial(_s5_scan, nb=nb, tc=tc)
    p_ref[...] = _s5_core(u_ref[...], bu_ref, st_ref, h0re_ref, h0im_ref, are_ref, aim_ref, bw_ref, cw_ref,
                          dsk_ref, sre_ref, sim_ref, scan).astype(p_ref.dtype)


def _s5_params(a_re, a_im, log_dt, b_re, b_im, c_re, c_im):
    g, p = a_re.shape
    dt = jnp.exp(log_dt)[:, None]
    mag = jnp.exp(dt * a_re)
    ang = dt * a_im
    ab_re = mag * jnp.cos(ang)
    ab_im = mag * jnp.sin(ang)
    n_re = ab_re - 1.0
    n_im = ab_im
    den = a_re * a_re + a_im * a_im
    f_re = ((n_re * a_re + n_im * a_im) / den)[..., None]
    f_im = ((n_im * a_re - n_re * a_im) / den)[..., None]
    bb_re = f_re * b_re - f_im * b_im
    bb_im = f_re * b_im + f_im * b_re
    ngb = g // SSM_GB
    eye = jnp.eye(SSM_GB, dtype=F32)

    def bdiag_b(bb):
        t = bb.reshape(ngb, SSM_GB, p, SSM_GROUP).transpose(0, 1, 3, 2)
        t = t[:, :, :, None, :] * eye[None, :, None, :, None]
        return t.reshape(ngb, SSM_CH, SSM_ST).astype(BF16)

    def bdiag_c(cc):
        t = cc.reshape(ngb, SSM_GB, SSM_GROUP, p).transpose(0, 1, 3, 2)
        t = t[:, :, :, None, :] * eye[None, :, None, :, None]
        return t.reshape(ngb, SSM_ST, SSM_CH).astype(BF16)

    bw =jnp.concatenate([bdiag_b(bb_re), bdiag_b(bb_im)], axis=2)
    cw = jnp.concatenate([bdiag_c(c_re), -bdiag_c(c_im)], axis=1)
    are = jnp.broadcast_to(ab_re.reshape(1, g * p), (8, g * p))
    aim = jnp.broadcast_to(ab_im.reshape(1, g * p), (8, g * p))
    return are, aim, bw, cw


def _s5_common_specs(nb):
    return [
        pl.BlockSpec((nb, SSM_ST), lambda gb, t: (0, gb)),
        pl.BlockSpec((nb, SSM_ST), lambda gb, t: (0, gb)),
        pl.BlockSpec((8, SSM_ST), lambda gb, t: (0, gb)),
        pl.BlockSpec((8, SSM_ST), lambda gb, t: (0, gb)),
        pl.BlockSpec((None, SSM_CH, 2 * SSM_ST), lambda gb, t: (gb, 0, 0)),
        pl.BlockSpec((None, 2 * SSM_ST, SSM_CH), lambda gb, t: (gb, 0, 0)),
        pl.BlockSpec((1, SSM_CH), lambda gb, t: (0, gb)),
    ]


def _s5_prompt(h, nbatch, lp, params, dsk):
    assert nbatch == 4, "the paired scan packs two steps of 4 sequences into one 8-row tile"
    are, aim, bw, cw = params
    d = h.shape[1]
    ngb = d // SSM_CH
    tc = 256
    nt = lp // tc
    gp = are.shape[1]
    zeros = jnp.zeros((8, gp), F32)
    u_specs = [pl.BlockSpec((tc, SSM_CH), functools.partial(lambda gb, t, b: (b * nt + t, gb), b=b))
               for b in range(nbatch)]
    rows = tc * nbatch
    return pl.pallas_call(
        functools.partial(_s5_prompt_kernel, nbatch=nbatch, tc=tc),
        out_shape=(jax.ShapeDtypeStruct((nbatch, lp, d), BF16),
                   jax.ShapeDtypeStruct((8, gp), F32), jax.ShapeDtypeStruct((8, gp), F32)),
        grid=(ngb, nt),
        in_specs=u_specs + _s5_common_specs(8),
        out_specs=(pl.BlockSpec((nbatch, tc, SSM_CH), lambda gb, t: (0, t, gb)),
                   pl.BlockSpec((8, SSM_ST), lambda gb, t: (0, gb)),
                   pl.BlockSpec((8, SSM_ST), lambda gb, t: (0, gb))),
        scratch_shapes=[pltpu.VMEM((SSM_CH // 128, rows, 128), F32), pltpu.VMEM((8, 2 * SSM_ST), F32),
                        pltpu.VMEM((SSM_CH // 128, rows, 128), F32)]
        + [pltpu.VMEM((rows // S5_NSUB, 2 * SSM_ST), F32)] * (2 * S5_NSUB),
        compiler_params=_cparams(("parallel", "arbitrary")),
        name="s5_prompt",
    )(*([h] * nbatch), zeros, zeros, are, aim, bw, cw, dsk)


def _s5_sample(h, mp, bs, ls, h0re, h0im, params, dsk):
    are, aim, bw, cw = params
    d = h.shape[1]
    ngb = d // SSM_CH
    rows = ls * bs
    gp = are.shape[1]
    blk0 = mp // rows
    return pl.pallas_call(
        functools.partial(_s5_sample_kernel, nb=bs, tc=ls),
        out_shape=(jax.ShapeDtypeStruct((rows, d), BF16),
                   jax.ShapeDtypeStruct((bs, gp), F32), jax.ShapeDtypeStruct((bs, gp), F32)),
        grid=(ngb, 1),
        in_specs=[pl.BlockSpec((rows, SSM_CH), lambda gb, t: (blk0, gb))] + _s5_common_specs(bs),
        out_specs=(pl.BlockSpec((rows, SSM_CH), lambda gb, t: (0, gb)),
                   pl.BlockSpec((bs, SSM_ST), lambda gb, t: (0, gb)),
                   pl.BlockSpec((bs, SSM_ST), lambda gb, t: (0, gb))),
        scratch_shapes=[pltpu.VMEM((rows, 2 * SSM_ST), F32), pltpu.VMEM((bs, 2 * SSM_ST), F32)],
        compiler_params=_cparams(("parallel", "arbitrary")),
        name="s5_sample",
    )(h, h0re, h0im, are, aim, bw, cw, dsk)


def _ln_silu(y, g, b):
    mu = jnp.mean(y, axis=-1, keepdims=True)
    yc = y - mu
    z = yc * lax.rsqrt(jnp.mean(yc * yc, axis=-1, keepdims=True) + EPS) * g + b
    return _silu(z)


def _conv_prompt_kernel(cur_ref, halo_ref, buf_ref, w_ref, *rest, width, mode):
    if mode == "cf":
        bdw_ref, lng_ref, lnb_ref, o_ref, s_ref, acc_ref = rest
    else:
        bg_ref, o_ref, s_ref, acc_ref = rest
    t = pl.program_id(1)
    hb = halo_ref.shape[0]
    tc, d = cur_ref.shape
    first = t == 0
    s_ref[0:hb, :] = jnp.where(first, buf_ref[0], halo_ref[...])
    s_ref[hb:hb + tc, :] = cur_ref[...]
    s_ref[hb + tc:hb + tc + 8, :] = jnp.zeros((8, d), F32)
    off = hb - (width - 1)
    by_shift = [[(o // 8, o - off) for o in range(off, off + width) if o % 8 == sh] for sh in range(8)]
    rc, lch = 64, 256
    for r0 in range(0, tc, rc):
        for c in range(d // lch):
            cols = pl.ds(c * lch, lch)
            out = None
            for sh, taps in enumerate(by_shift):
                z = None
                for q, k in taps:
                    t = w_ref[k:k + 1, cols] * s_ref[r0 + 8 * q:r0 + 8 * q + rc + 8, cols]
                    z = t if z is None else z + t
                if z is not None:
                    zs = z[sh:sh + rc]
                    out = zs if out is None else out + zs
            acc_ref[r0:r0 + rc, cols] = out
    if mode == "cf":
        o_ref[...] = _ln_silu(acc_ref[...] + bdw_ref[...], lng_ref[...], lnb_ref[...]).astype(o_ref.dtype)
    else:
        o_ref[...] = (bg_ref[...] * acc_ref[...]).astype(o_ref.dtype)


def _conv_prompt(v, nbatch, lp, w, layer, mode, extra):
    d = v.shape[1]
    width = w.shape[1]
    hb = 32 if width > 9 else 8
    tc = 256
    nt = lp // tc
    zeros = jnp.zeros((nbatch, hb, d), F32)
    in_specs = [
        pl.BlockSpec((tc, d), lambda b, t: (b * nt + t, 0)),
        pl.BlockSpec((hb, d), lambda b, t: (jnp.maximum((b * lp + t * tc) // hb - 1, 0), 0)),
        pl.BlockSpec((1, hb, d), lambda b, t: (b, 0, 0)),
        pl.BlockSpec((None, width, d), lambda b, t: (layer, 0, 0)),
    ]
    if mode == "cf":
        in_specs += [pl.BlockSpec((1, d), lambda b, t: (layer, 0))] * 3
    else:
        in_specs += [pl.BlockSpec((tc, d), lambda b, t: (b * nt + t, 0))]
    return pl.pallas_call(
        functools.partial(_conv_prompt_kernel, width=width, mode=mode),
        out_shape=jax.ShapeDtypeStruct((nbatch * lp, d), BF16),
        grid=(nbatch, nt),
        in_specs=in_specs,
        out_specs=pl.BlockSpec((tc, d), lambda b, t: (b * nt + t, 0)),
        scratch_shapes=[pltpu.VMEM((hb + tc + 8, d), F32), pltpu.VMEM((tc, d), F32)],
        compiler_params=_cparams(("parallel", "arbitrary")),
        name="conv_prompt_" + mode,
    )(v, v, zeros, w, *extra)


def _conv_sample_kernel(v_ref, cache_ref, w_ref, *rest, width, mode):
    if mode == "cf":
        bdw_ref, lng_ref, lnb_ref, o_ref = rest
    else:
        bg_ref, o_ref = rest
    ls = v_ref.shape[0]
    hist = width - 1

    def full(j):
        return cache_ref[j] if j < hist else v_ref[j - hist]

    for l in range(ls):
        acc = w_ref[0:1, :] * full(l)
        for k in range(1, width):
            acc = acc + w_ref[k:k + 1, :] * full(l + k)
        if mode == "cf":
            o_ref[l] = _ln_silu(acc + bdw_ref[...], lng_ref[...], lnb_ref[...]).astype(o_ref.dtype)
        else:
            o_ref[l] = (bg_ref[l] * acc).astype(o_ref.dtype)


def _conv_sample(v_tm, cache_tm, w, layer, mode, extra):
    ls, bs, d = v_tm.shape
    width = w.shape[1]
    bc = 16
    in_specs = [
        pl.BlockSpec((ls, bc, d), lambda i: (0, i, 0)),
        pl.BlockSpec((width - 1, bc, d), lambda i: (0, i, 0)),
        pl.BlockSpec((None, width, d), lambda i: (layer, 0, 0)),
    ]
    if mode == "cf":
        in_specs += [pl.BlockSpec((1, d), lambda i: (layer, 0))] * 3
    else:
        in_specs += [pl.BlockSpec((ls, bc, d), lambda i: (0, i, 0))]
    return pl.pallas_call(
        functools.partial(_conv_sample_kernel, width=width, mode=mode),
        out_shape=jax.ShapeDtypeStruct((ls, bs, d), BF16),
        grid=(bs // bc,),
        in_specs=in_specs,
        out_specs=pl.BlockSpec((ls, bc, d), lambda i: (0, i, 0)),
        compiler_params=_cparams(("parallel",)),
        name="conv_sample_" + mode,
    )(v_tm, cache_tm, w, *extra)


def _onehot2(r):
    lane = lax.broadcasted_iota(I32, r.shape, 1).astype(F32)
    return jnp.where((lane == r[:, 0:1]) | (lane == r[:, 1:2]), 1.0, 0.0)


def _moe_plan_kernel(r_ref, pos_ref, info_ref, cnt_ref, run_ref, pst_ref):
    ph = pl.program_id(0)
    i = pl.program_id(1)
    r = r_ref[...]
    tb = r.shape[0]
    oh = _onehot2(r)
    colsum = jnp.broadcast_to(jnp.sum(oh, axis=0, keepdims=True), (8, ROUTE_LANES))

    @pl.when((ph == 0) & (i == 0))
    def _():
        cnt_ref[...] = jnp.zeros_like(cnt_ref)

    @pl.when(ph == 0)
    def _():
        cnt_ref[...] += colsum

    @pl.when((ph == 1) & (i == 0))
    def _():
        cnt = cnt_ref[...]
        nblk = jnp.floor((cnt + (MOE_BM - 1)) * (1.0 / MOE_BM))
        li = lax.broadcasted_iota(I32, (ROUTE_LANES, ROUTE_LANES), 0)
        lj = lax.broadcasted_iota(I32, (ROUTE_LANES, ROUTE_LANES), 1)
        upper = jnp.where(li <= lj, 1.0, 0.0).astype(BF16)
        bend = _dot(nblk.astype(BF16), upper)
        pst_ref[...] = (bend - nblk) * MOE_BM
        run_ref[...] = jnp.zeros_like(run_ref)
        n_used = bend[0:1, N_EXPERTS - 1:N_EXPERTS]
        bi = lax.broadcasted_iota(I32, (PLAN_ROWS, ROUTE_LANES), 0).astype(F32)
        ln = lax.broadcasted_iota(I32, (PLAN_ROWS, ROUTE_LANES), 1)
        ln_f = ln.astype(F32)
        pe = jnp.broadcast_to(bend[0:1, :], (PLAN_ROWS, ROUTE_LANES))
        pk = jnp.broadcast_to(nblk[0:1, :], (PLAN_ROWS, ROUTE_LANES))
        bcl = jnp.minimum(bi, n_used - 1.0)
        blk_e = jnp.sum(jnp.where((ln < N_EXPERTS) & (pe <= bcl), 1.0, 0.0), axis=1, keepdims=True)
        bend_col = jnp.sum(jnp.where(ln_f == bi, pe, 0.0), axis=1, keepdims=True)
        nblk_col = jnp.sum(jnp.where(ln_f == bi, pk, 0.0), axis=1, keepdims=True)
        info = jnp.where(ln == 0, blk_e, 0.0)
        info = jnp.where(ln == 1, n_used, info)
        info = jnp.where(ln == 2, bend_col, info)
        info = jnp.where(ln == 3, nblk_col, info)
        nonempty = (ln < N_EXPERTS) & (pk > 0.0)
        seq = jnp.sum(jnp.where(nonempty & (ln_f < blk_e), 1.0, 0.0), axis=1, keepdims=True)
        nxt = jnp.min(jnp.where(nonempty & (ln_f > blk_e), ln_f, float(ROUTE_LANES)), axis=1, keepdims=True)
        info = jnp.where(ln == 4, seq, info)
        info = jnp.where(ln == 5, nxt, info)
        info_ref[...] = info.astype(I32)

    @pl.when(ph == 1)
    def _():
        ti = lax.broadcasted_iota(I32, (tb, tb), 0)
        tj = lax.broadcasted_iota(I32, (tb, tb), 1)
        lower = jnp.where(ti > tj, 1.0, 0.0).astype(BF16)
        before = _dot(lower, oh.astype(BF16))
        val = pst_ref[0:1, :] + run_ref[0:1, :] + before
        lane = lax.broadcasted_iota(I32, r.shape, 1)
        lane_f = lane.astype(F32)
        p0 = jnp.sum(jnp.where(lane_f == r[:, 0:1], val, 0.0), axis=1, keepdims=True)
        p1 = jnp.sum(jnp.where(lane_f == r[:, 1:2], val, 0.0), axis=1, keepdims=True)
        pos_ref[...] = jnp.where(lane == 0, p0, jnp.where(lane == 1, p1, 0.0)).astype(I32)
        run_ref[...] += colsum


def _moe_plan(route):
    m = route.shape[0]
    nb = m // PLAN_TB
    return pl.pallas_call(
        _moe_plan_kernel,
        out_shape=(jax.ShapeDtypeStruct((m, ROUTE_LANES), I32), jax.ShapeDtypeStruct((PLAN_ROWS, ROUTE_LANES), I32)),
        grid=(2, nb),
        in_specs=[pl.BlockSpec((PLAN_TB, ROUTE_LANES), lambda ph, i: (i, 0))],
        out_specs=(pl.BlockSpec((PLAN_TB, ROUTE_LANES), lambda ph, i: (ph * i, 0)),
                   pl.BlockSpec((PLAN_ROWS, ROUTE_LANES), lambda ph, i: (0, 0))),
        scratch_shapes=[pltpu.VMEM((8, ROUTE_LANES), F32)] * 3,
        compiler_params=_cparams(("arbitrary", "arbitrary")),
        name="moe_plan",
    )(route)


def _moe_scatter_kernel(pos_ref, bend_ref, nblk_ref, nu_ref, h_ref, xs_ref, z_ref, sem, zsem):
    i = pl.program_id(0)
    n_blocks = xs_ref.shape[0] // MOE_BM
    rows = h_ref.shape[0]

    @pl.when(i == 0)
    def _():
        z_ref[...] = jnp.zeros_like(z_ref)

        def zero_copy(blk):
            start = pl.multiple_of(blk * MOE_BM, MOE_BM)
            return pltpu.make_async_copy(z_ref, xs_ref.at[pl.ds(start, MOE_BM)], zsem)

        def for_each_pad_block(fn):
            def per_expert(e, c):
                @pl.when(nblk_ref[e] > 0)
                def _():
                    fn(zero_copy(bend_ref[e] - 1))
                return c

            def per_tail_block(blk, c):
                @pl.when(blk >= nu_ref[0])
                def _():
                    fn(zero_copy(blk))
                return c

            lax.fori_loop(0, N_EXPERTS, per_expert, 0)
            lax.fori_loop(0, n_blocks, per_tail_block, 0)

        for_each_pad_block(lambda c: c.start())
        for_each_pad_block(lambda c: c.wait())

    def row(r, c):
        base = (i * rows + r) * 2
        for k in range(2):
            p = pos_ref[base + k]
            pltpu.make_async_copy(h_ref.at[pl.ds(r, 1)], xs_ref.at[pl.ds(p, 1)], sem).start()
        return c

    lax.fori_loop(0, rows, row, 0, unroll=4)
    for k in range(2):
        pltpu.make_async_copy(h_ref, xs_ref.at[pl.ds(0, rows)], sem).wait()


def _moe_scatter(h, pos2, bend, nblk, n_used, n_rows):
    m, d = h.shape
    grid_spec = pltpu.PrefetchScalarGridSpec(
        num_scalar_prefetch=4,
        grid=(m // TMS,),
        in_specs=[pl.BlockSpec((TMS, d), lambda i, *_: (i, 0))],
        out_specs=pl.BlockSpec(memory_space=pl.ANY),
        scratch_shapes=[pltpu.VMEM((MOE_BM, d), F32), pltpu.SemaphoreType.DMA(()), pltpu.SemaphoreType.DMA(())],
    )
    return pl.pallas_call(
        _moe_scatter_kernel,
        out_shape=jax.ShapeDtypeStruct((n_rows, d), F32),
        grid_spec=grid_spec,
        compiler_params=_cparams(("arbitrary",)),
        name="moe_scatter",
    )(pos2, bend, nblk, n_used, h)


def _moe_expert_kernel(be_ref, nu_ref, seq_ref, nxt_ref, xs_ref, w13_hbm, w2_hbm, y_ref,
                       w13f_ref, w2f_ref, w13b_ref, w2b_ref, sem13, sem2, *, layer):
    i = pl.program_id(0)
    used = i < nu_ref[0]
    e = be_ref[i]
    new_expert = (i == 0) | (e != be_ref[jnp.maximum(i - 1, 0)])
    slot = seq_ref[i] % 2

    def fetch(expert, s):
        return (pltpu.make_async_copy(w13_hbm.at[layer, expert], w13f_ref.at[s], sem13.at[s]),
                pltpu.make_async_copy(w2_hbm.at[layer, expert], w2f_ref.at[s], sem2.at[s]))

    @pl.when(i == 0)
    def _():
        for c in fetch(e, slot):
            c.start()

    @pl.when(used & new_expert)
    def _():
        nxt = nxt_ref[i]

        @pl.when(nxt < N_EXPERTS)
        def _():
            for c in fetch(nxt, 1 - slot):
                c.start()

        for c in fetch(e, slot):
            c.wait()
        w13b_ref[...] = w13f_ref[slot].astype(BF16)
        w2b_ref[...] = w2f_ref[slot].astype(BF16)

    @pl.when(used)
    def _():
        ab = _dot(xs_ref[...].astype(BF16), w13b_ref[...])
        a = ab[:, 0:D_EXPERT]
        b = ab[:, D_EXPERT:2 * D_EXPERT]
        y_ref[...] = _dot((_silu(a) * b).astype(BF16), w2b_ref[...])

    @pl.when(jnp.logical_not(used))
    def _():
        y_ref[...] = jnp.zeros_like(y_ref)


def _moe_experts(xs, blk_e, n_used, seq, nxt, w13, w2, layer):
    n_rows, d = xs.shape
    n_blocks = blk_e.shape[0]
    de2 = w13.shape[3]
    grid_spec = pltpu.PrefetchScalarGridSpec(
        num_scalar_prefetch=4,
        grid=(n_blocks,),
        in_specs=[
            pl.BlockSpec((MOE_BM, d), lambda i, be, nu, sq, nx: (jnp.minimum(i, nu[0] - 1), 0)),
            pl.BlockSpec(memory_space=pl.ANY),
            pl.BlockSpec(memory_space=pl.ANY),
        ],
        out_specs=pl.BlockSpec((MOE_BM, d), lambda i, be, nu, sq, nx: (i, 0)),
        scratch_shapes=[pltpu.VMEM((2, d, de2), F32), pltpu.VMEM((2, de2 // 2, d), F32),
                        pltpu.VMEM((d, de2), BF16), pltpu.VMEM((de2 // 2, d), BF16),
                        pltpu.SemaphoreType.DMA((2,)), pltpu.SemaphoreType.DMA((2,))],
    )
    return pl.pallas_call(
        functools.partial(_moe_expert_kernel, layer=layer),
        out_shape=jax.ShapeDtypeStruct((n_rows, d), F32),
        grid_spec=grid_spec,
        compiler_params=_cparams(("arbitrary",)),
        name="moe_experts",
    )(blk_e, n_used, seq, nxt, xs, w13, w2)


def _moe_combine_kernel(pos_ref, y_ref, x_ref, r_ref, gp_ref, gs_ref, o_ref, ybuf, sem, *, npb, bpb, nxt_refs=()):
    i = pl.program_id(0)
    n = pl.num_programs(0)
    slot = i % 2

    def gather(blk, s):
        def row(r, c):
            base = (blk * TMC + r) * 2
            for k in range(2):
                p = pos_ref[base + k]
                pltpu.make_async_copy(y_ref.at[pl.ds(p, 1)], ybuf.at[s, k, pl.ds(r, 1)], sem.at[s]).start()
            return c
        lax.fori_loop(0, TMC, row, 0, unroll=4)

    @pl.when(i == 0)
    def _():
        gather(0, 0)

    @pl.when(i + 1 < n)
    def _():
        gather(i + 1, 1 - slot)

    for k in range(2):
        pltpu.make_async_copy(y_ref.at[pl.ds(0, TMC)], ybuf.at[slot, k], sem.at[slot]).wait()

    r = r_ref[...]
    f = r[:, 2:3] * ybuf[slot, 0] + r[:, 3:4] * ybuf[slot, 1]
    gate = _seq_vec(gp_ref, gs_ref, i, npb, bpb, TMC)
    x_new = x_ref[...] + gate * f
    o_ref[...] = x_new
    if nxt_refs:
        g_ref, shp_ref, scp_ref, shs_ref, scs_ref, h_ref = nxt_refs
        sc = _seq_vec(scp_ref, scs_ref, i, npb, bpb, TMC)
        sh = _seq_vec(shp_ref, shs_ref, i, npb, bpb, TMC)
        h_ref[...] = (_rms(x_new, g_ref[...]) * (1.0 + sc) + sh).astype(h_ref.dtype)


def _moe_combine_kernel_fused(pos_ref, y_ref, x_ref, r_ref, gp_ref, gs_ref, g_ref, shp_ref, scp_ref, shs_ref,
                              scs_ref, o_ref, h_ref, ybuf, sem, *, npb, bpb):
    _moe_combine_kernel(pos_ref, y_ref, x_ref, r_ref, gp_ref, gs_ref, o_ref, ybuf, sem, npb=npb, bpb=bpb,
                        nxt_refs=(g_ref, shp_ref, scp_ref, shs_ref, scs_ref, h_ref))


def _moe_combine(y, pos2, x, route, ap, as_, gate_idx, mp, lp, nxt=None):
    m, d = x.shape
    bs = as_.shape[0]
    in_specs = [
        pl.BlockSpec(memory_space=pl.ANY),
        pl.BlockSpec((TMC, d), lambda i, pos: (i, 0)),
        pl.BlockSpec((TMC, ROUTE_LANES), lambda i, pos: (i, 0)),
        pl.BlockSpec((8, d), lambda i, pos: (0, gate_idx)),
        pl.BlockSpec((bs, d), lambda i, pos: (0, gate_idx)),
    ]
    out_shape = jax.ShapeDtypeStruct((m, d), F32)
    out_specs = pl.BlockSpec((TMC, d), lambda i, pos: (i, 0))
    args = (pos2, y, x, route, ap, as_)
    kernel_fn = _moe_combine_kernel
    if nxt is not None:
        g_n, ap_n, as_n, dtype_n = nxt
        in_specs += [
            pl.BlockSpec((1, d), lambda i, pos: (0, 0)),
            pl.BlockSpec((8, d), lambda i, pos: (0, 0)),
            pl.BlockSpec((8, d), lambda i, pos: (0, 1)),
            pl.BlockSpec((bs, d), lambda i, pos: (0, 0)),
            pl.BlockSpec((bs, d), lambda i, pos: (0, 1)),
        ]
        out_shape = (out_shape, jax.ShapeDtypeStruct((m, d), dtype_n))
        out_specs = (out_specs, pl.BlockSpec((TMC, d), lambda i, pos: (i, 0)))
        args += (g_n, ap_n, ap_n, as_n, as_n)
        kernel_fn = _moe_combine_kernel_fused
    grid_spec = pltpu.PrefetchScalarGridSpec(
        num_scalar_prefetch=1,
        grid=(m // TMC,),
        in_specs=in_specs,
        out_specs=out_specs,
        scratch_shapes=[pltpu.VMEM((2, 2, TMC, d), F32), pltpu.SemaphoreType.DMA((2,))],
    )
    return pl.pallas_call(
        functools.partial(kernel_fn, npb=mp // TMC, bpb=lp // TMC),
        out_shape=out_shape,
        grid_spec=grid_spec,
        compiler_params=_cparams(("arbitrary",)),
        name="moe_combine",
    )(*args)


def _moe_layer(x, h2, route, w13, w2, layer, ap, as_, mp, lp, nxt=None):
    m = x.shape[0]
    n_blocks = (2 * m + N_EXPERTS * (MOE_BM - 1) + MOE_BM - 1) // MOE_BM
    assert n_blocks <= PLAN_ROWS and m % PLAN_TB == 0 and m % TMC == 0 and m % TMS == 0
    pos, info = _moe_plan(route)
    pos2 = pos[:, 0:2].reshape(2 * m)
    blk_e, n_used = info[:n_blocks, 0], info[0:1, 1]
    bend, nblk = info[:N_EXPERTS, 2], info[:N_EXPERTS, 3]
    seq, nxt_e = info[:n_blocks, 4], info[:n_blocks, 5]
    xs = _moe_scatter(h2, pos2, bend, nblk, n_used, n_blocks * MOE_BM)
    y = _moe_experts(xs, blk_e, n_used, seq, nxt_e, w13, w2, layer)
    return _moe_combine(y, pos2, x, route, ap, as_, 5, mp, lp, nxt=nxt)


def kernel(x_prompt, x_sample, c_prompt, c_sample, state_l0_ssm_re, state_l0_ssm_im, cache_l1_conformer_conv, cache_l2_short_conv, state_l3_ssm_re, state_l3_ssm_im, norm1_g, norm2_g, w_ada, b_ada, final_norm_g, ssm_a_re, ssm_a_im, ssm_log_dt, ssm_b_re, ssm_b_im, ssm_c_re, ssm_c_im, ssm_d, ssm_w_glu, cf_w_pw1, cf_w_dw, cf_b_dw, cf_ln_g, cf_ln_b, cf_w_pw2, sc_w_in, sc_w_conv, sc_w_out, moe_w_group, moe_b_group, moe_w_expert, moe_b_expert, moe_w13, moe_w2):
    bp, lp, d = x_prompt.shape
    bs, ls, _ = x_sample.shape
    depth = w_ada.shape[0]
    mp, ms = bp * lp, bs * ls
    g_ssm, p_ssm = ssm_a_re.shape[1:]
    assert d == D_MODEL and lp % TM == 0 and ms % TM == 0 and TM % bs == 0 and bp <= 8 and bs % 16 == 0
    assert TMC % bs == 0 and mp % ms == 0

    x = jnp.concatenate([x_prompt.reshape(mp, d), x_sample.transpose(1, 0, 2).reshape(ms, d)], axis=0)
    c_all = jnp.concatenate([c_prompt, jnp.zeros((8 - bp, d), F32), c_sample], axis=0)
    ada = _ada_all(c_all, w_ada, b_ada)
    n_pad = ROUTE_LANES - N_GROUPS - N_EXPERTS
    wr_all = jnp.concatenate([moe_w_group, moe_w_expert, jnp.zeros((depth, d, n_pad), F32)], axis=2)
    br_all = jnp.concatenate([moe_b_group, moe_b_expert, jnp.zeros((depth, n_pad), F32)], axis=1)

    sample_init = [(state_l0_ssm_re, state_l0_ssm_im), (cache_l1_conformer_conv,), (cache_l2_short_conv,),
                   (state_l3_ssm_re, state_l3_ssm_im)]
    p_states, s_states = [], []

    def mixer_in_dtype(layer):
        return F32 if layer % 3 == 0 else BF16

    h = _modulate(x, norm1_g[0:1], ada[0, 0:8], ada[0, 8:], 0, 1, mp, lp, mixer_in_dtype(0))
    for i in range(depth):
        ap, as_ = ada[i, 0:8], ada[i, 8:]
        kind, j = i % 3, i // 3
        if kind == 0:
            params = _s5_params(ssm_a_re[j], ssm_a_im[j], ssm_log_dt[j], ssm_b_re[j], ssm_b_im[j],
                                ssm_c_re[j], ssm_c_im[j])
            dsk = ssm_d[j:j + 1]
            pp, pre, pim = _s5_prompt(h, bp, lp, params, dsk)
            h0re, h0im = sample_init[i]
            psm, sre, sim = _s5_sample(h, mp, bs, ls, h0re.reshape(bs, g_ssm * p_ssm),
                                       h0im.reshape(bs, g_ssm * p_ssm), params, dsk)
            p_states.append((pre[:bp].reshape(bp, g_ssm, p_ssm), pim[:bp].reshape(bp, g_ssm, p_ssm)))
            s_states.append((sre.reshape(bs, g_ssm, p_ssm), sim.reshape(bs, g_ssm, p_ssm)))
            pre_out = (pp.reshape(mp, d), psm)
            w_out = ssm_w_glu
        elif kind == 1:
            gl = _mm_split((h,), cf_w_pw1, j, 2, _mm_glu_kernel, 1, 512, mp, "mm_glu")
            extra = (cf_b_dw, cf_ln_g, cf_ln_b)
            pp = _conv_prompt(gl, bp, lp, cf_w_dw, j, "cf", extra)
            (cache,) = sample_init[i]
            g_tm = gl[mp:].reshape(ls, bs, d)
            psm = _conv_sample(g_tm, cache.transpose(1, 0, 2), cf_w_dw, j, "cf", extra)
            hist = CF_WIDTH - 1
            p_states.append((jnp.stack([gl[b * lp + lp - hist:(b + 1) * lp] for b in range(bp)]),))
            s_states.append((jnp.concatenate([cache, g_tm.transpose(1, 0, 2)], axis=1)[:, -hist:],))
            pre_out = (pp, psm.reshape(ms, d))
            w_out = cf_w_pw2
        else:
            bg, v = _mm_split((h,), sc_w_in, j, 3, _mm_bcv_kernel, 2, 512, mp, "mm_bcv")
            pp = _conv_prompt(v, bp, lp, sc_w_conv, j, "sc", (bg,))
            (cache,) = sample_init[i]
            v_tm = v[mp:].reshape(ls, bs, d)
            psm = _conv_sample(v_tm, cache.transpose(1, 0, 2), sc_w_conv, j, "sc", (bg[mp:].reshape(ls, bs, d),))
            hist = SC_WIDTH - 1
            p_states.append((jnp.stack([v[b * lp + lp - hist:(b + 1) * lp] for b in range(bp)]),))
            s_states.append((jnp.concatenate([cache, v_tm.transpose(1, 0, 2)], axis=1)[:, -hist:],))
            pre_out = (pp, psm.reshape(ms, d))
            w_out = sc_w_out

        x = _mm_resid(pre_out, w_out, j, x, ap, as_, 2, mp, lp, glu=(kind == 0))

        h2, route = _modulate(x, norm2_g[i:i + 1], ap, as_, 3, 4, mp, lp, F32, router=(wr_all[i], br_all[i:i + 1]))
        if i + 1 < depth:
            nxt = (norm1_g[i + 1:i + 2], ada[i + 1, 0:8], ada[i + 1, 8:], mixer_in_dtype(i + 1))
            x, h = _moe_layer(x, h2, route, moe_w13, moe_w2, i, ap, as_, mp, lp, nxt=nxt)
        else:
            x = _moe_layer(x, h2, route, moe_w13, moe_w2, i, ap, as_, mp, lp)

    y_p, y_s = _final_norm(x, final_norm_g.reshape(1, d), mp)
    y_prompt = y_p.reshape(bp, lp, d)
    y_sample = y_s.reshape(ls, bs, d).transpose(1, 0, 2)
    ps, ss = p_states, s_states
    return (y_prompt, y_sample,
            ps[0][0], ps[0][1], ps[1][0], ps[2][0], ps[3][0], ps[3][1],
            ss[0][0], ss[0][1], ss[1][0], ss[2][0], ss[3][0], ss[3][1])
```

```python
import functools

import jax
import jax.numpy as jnp
from jax import lax
from jax.experimental import pallas as pl
from jax.experimental.pallas import tpu as pltpu

F32 = jnp.float32
BF16 = jnp.bfloat16
I32 = jnp.int32

D_MODEL = 2048
SSM_GROUP = 16
SSM_STATE = 64
SSM_GB = 16
SSM_CH = SSM_GB * SSM_GROUP
SSM_ST = SSM_GB * SSM_STATE
CF_WIDTH = 31
SC_WIDTH = 3
N_GROUPS = 8
EPG = 8
N_EXPERTS = N_GROUPS * EPG
D_EXPERT = D_MODEL // 4
EPS = 1e-6

TM = 512
MOE_BM = 128
TMC = 256
TMS = 512
PLAN_TB = 512
PLAN_ROWS = 256
ROUTE_LANES = 128
VMEM_LIMIT = 56 * 1024 * 1024


def _cparams(sem):
    return pltpu.CompilerParams(dimension_semantics=sem, vmem_limit_bytes=VMEM_LIMIT)


def _dot(a, b):
    return jnp.dot(a, b, preferred_element_type=F32)


def _dotb(a, b):
    return _dot(a.astype(BF16), b.astype(BF16))


def _sigmoid(x):
    return 1.0 / (1.0 + jnp.exp(-x))


def _silu(x):
    return x * _sigmoid(x)


def _gelu_tanh(x):
    return 0.5 * x * (1.0 + jnp.tanh(0.7978845608028654 * (x + 0.044715 * (x * x * x))))


def _seq_vec(vp_ref, vs_ref, blk, n_prompt_blocks, blocks_per_batch, rows):
    b = jnp.minimum(blk // blocks_per_batch, vp_ref.shape[0] - 1)
    vp = vp_ref[pl.ds(b, 1), :]
    vs = vs_ref[...]
    reps = rows // vs.shape[0]
    if reps > 1:
        vs = jnp.concatenate([vs] * reps, axis=0)
    return jnp.where(blk < n_prompt_blocks, vp, vs)


def _ada_kernel(c_ref, w_ref, b_ref, o_ref):
    c = c_ref[...]
    o_ref[0] = _dotb(_silu(c), w_ref[0]) + b_ref[0]


def _ada_all(c_all, w_ada, b_ada):
    depth, d, n = w_ada.shape
    rows = c_all.shape[0]
    tn = 1024
    return pl.pallas_call(
        _ada_kernel,
        out_shape=jax.ShapeDtypeStruct((depth, rows, n), F32),
        grid=(depth, n // tn),
        in_specs=[
            pl.BlockSpec((rows, d), lambda l, j: (0, 0)),
            pl.BlockSpec((1, d, tn), lambda l, j: (l, 0, j)),
            pl.BlockSpec((1, 1, tn), lambda l, j: (l, 0, j)),
        ],
        out_specs=pl.BlockSpec((1, rows, tn), lambda l, j: (l, 0, j)),
        compiler_params=_cparams(("parallel", "parallel")),
        name="ada_proj",
    )(c_all, w_ada, b_ada.reshape(depth, 1, n))


def _rms(x, g):
    return x * lax.rsqrt(jnp.mean(x * x, axis=-1, keepdims=True) + EPS) * g


def _modulate_kernel(x_ref, g_ref, shp_ref, scp_ref, shs_ref, scs_ref, o_ref, *, npb, bpb):
    i = pl.program_id(0)
    rows = x_ref.shape[0]
    y = _rms(x_ref[...], g_ref[...])
    sc = _seq_vec(scp_ref, scs_ref, i, npb, bpb, rows)
    sh = _seq_vec(shp_ref, shs_ref, i, npb, bpb, rows)
    o_ref[...] = (y * (1.0 + sc) + sh).astype(o_ref.dtype)


def _route_from_logits(lg):
    rows = lg.shape[0]
    lane = lax.broadcasted_iota(I32, (rows, ROUTE_LANES), 1)
    lane_f = lane.astype(F32)
    neg = jnp.float32(-jnp.inf)
    big = jnp.float32(ROUTE_LANES)
    is_g = lane < N_GROUPS
    gl = jnp.where(is_g, lg, neg)
    gmax = jnp.max(gl, axis=1, keepdims=True)
    grp = jnp.min(jnp.where(gl == gmax, lane_f, big), axis=1, keepdims=True)
    p_grp = 1.0 / jnp.sum(jnp.where(is_g, jnp.exp(gl - gmax), 0.0), axis=1, keepdims=True)
    e_grp = ((lane - N_GROUPS) >> 3).astype(F32)
    valid = (lane >= N_GROUPS) & (lane < N_GROUPS + N_EXPERTS) & (e_grp == grp)
    el = jnp.where(valid, lg, neg)
    v1 = jnp.max(el, axis=1, keepdims=True)
    i1 = jnp.min(jnp.where(el == v1, lane_f, big), axis=1, keepdims=True)
    el2 = jnp.where(lane_f == i1, neg, el)
    v2 = jnp.max(el2, axis=1, keepdims=True)
    i2 = jnp.min(jnp.where(el2 == v2, lane_f, big), axis=1, keepdims=True)
    t = jnp.exp(v2 - v1)
    den = 1.0 + t
    w1 = (1.0 / den) * p_grp
    w2 = (t / den) * p_grp
    out = jnp.where(lane == 0, i1 - N_GROUPS, 0.0)
    out = jnp.where(lane == 1, i2 - N_GROUPS, out)
    out = jnp.where(lane == 2, w1, out)
    out = jnp.where(lane == 3, w2, out)
    return out


def _modulate_route_kernel(x_ref, g_ref, shp_ref, scp_ref, shs_ref, scs_ref, wr_ref, br_ref,
                           o_ref, r_ref, *, npb, bpb):
    i = pl.program_id(0)
    rows = x_ref.shape[0]
    y = _rms(x_ref[...], g_ref[...])
    sc = _seq_vec(scp_ref, scs_ref, i, npb, bpb, rows)
    sh = _seq_vec(shp_ref, shs_ref, i, npb, bpb, rows)
    h = y * (1.0 + sc) + sh
    o_ref[...] = h
    lg = _dotb(h, wr_ref[...]) + br_ref[...]
    r_ref[...] = _route_from_logits(lg)


def _modulate(x, g, ap, as_, sh_idx, sc_idx, mp, lp, out_dtype, router=None):
    m, d = x.shape
    bs = as_.shape[0]
    npb, bpb = mp // TM, lp // TM
    in_specs = [
        pl.BlockSpec((TM, d), lambda i: (i, 0)),
        pl.BlockSpec((1, d), lambda i: (0, 0)),
        pl.BlockSpec((8, d), lambda i: (0, sh_idx)),
        pl.BlockSpec((8, d), lambda i: (0, sc_idx)),
        pl.BlockSpec((bs, d), lambda i: (0, sh_idx)),
        pl.BlockSpec((bs, d), lambda i: (0, sc_idx)),
    ]
    if router is None:
        return pl.pallas_call(
            functools.partial(_modulate_kernel, npb=npb, bpb=bpb),
            out_shape=jax.ShapeDtypeStruct((m, d), out_dtype),
            grid=(m // TM,),
            in_specs=in_specs,
            out_specs=pl.BlockSpec((TM, d), lambda i: (i, 0)),
            compiler_params=_cparams(("parallel",)),
            name="modulate",
        )(x, g, ap, ap, as_, as_)
    wr, br = router
    return pl.pallas_call(
        functools.partial(_modulate_route_kernel, npb=npb, bpb=bpb),
        out_shape=(jax.ShapeDtypeStruct((m, d), F32), jax.ShapeDtypeStruct((m, ROUTE_LANES), F32)),
        grid=(m // TM,),
        in_specs=in_specs + [
            pl.BlockSpec((d, ROUTE_LANES), lambda i: (0, 0)),
            pl.BlockSpec((1, ROUTE_LANES), lambda i: (0, 0)),
        ],
        out_specs=(pl.BlockSpec((TM, d), lambda i: (i, 0)), pl.BlockSpec((TM, ROUTE_LANES), lambda i: (i, 0))),
        compiler_params=_cparams(("parallel",)),
        name="modulate_route",
    )(x, g, ap, ap, as_, as_, wr, br)


def _a_specs(a_parts, npb):
    k = a_parts[0].shape[1]
    if len(a_parts) == 1:
        return [pl.BlockSpec((TM, k), lambda j, i: (i, 0))]
    return [pl.BlockSpec((TM, k), lambda j, i: (jnp.minimum(i, npb - 1), 0)),
            pl.BlockSpec((TM, k), lambda j, i: (jnp.maximum(i - npb, 0), 0))]


def _mm_products(a_refs, w_refs, wbf_ref, npb):
    i = pl.program_id(1)

    @pl.when(i == 0)
    def _():
        for s, w_ref in enumerate(w_refs):
            wbf_ref[s] = w_ref[...].astype(BF16)

    if len(a_refs) == 1:
        a = a_refs[0][...]
    else:
        a = jnp.where(i < npb, a_refs[0][...], a_refs[1][...])
    a = a.astype(BF16)
    return [_dot(a, wbf_ref[s]) for s in range(len(w_refs))]


def _mm_glu_kernel(*refs, n_a, npb):
    a_refs, (wa_ref, wb_ref, o_ref, wbf_ref) = refs[:n_a], refs[n_a:]
    za, zb = _mm_products(a_refs, (wa_ref, wb_ref), wbf_ref, npb)
    o_ref[...] = za * _sigmoid(zb)


def _mm_bcv_kernel(*refs, n_a, npb):
    a_refs, (w0_ref, w1_ref, w2_ref, bg_ref, v_ref, wbf_ref) = refs[:n_a], refs[n_a:]
    bg, cg, hv = _mm_products(a_refs, (w0_ref, w1_ref, w2_ref), wbf_ref, npb)
    bg_ref[...] = bg
    v_ref[...] = cg * hv


def _mm_resid_kernel(*refs, n_a, npb, bpb, glu):
    a_refs, rest = refs[:n_a], refs[n_a:]
    n_w = 2 if glu else 1
    w_refs = rest[:n_w]
    x_ref, gp_ref, gs_ref, o_ref, wbf_ref = rest[n_w:]
    z = _mm_products(a_refs, w_refs, wbf_ref, npb)
    z = z[0] * _sigmoid(z[1]) if glu else z[0]
    gate = _seq_vec(gp_ref, gs_ref, pl.program_id(1), npb, bpb, x_ref.shape[0])
    o_ref[...] = x_ref[...] + gate * z


def _w_spec(k, tn, layer, col0):
    return pl.BlockSpec((None, k, tn), lambda j, i: (layer, 0, col0 + j))


def _mm_split(a_parts, w, layer, n_split, kernel_fn, n_out, tn, mp, name):
    m = sum(a.shape[0] for a in a_parts)
    k = a_parts[0].shape[1]
    n = w.shape[2] // n_split
    nj = n // tn
    out_shape = [jax.ShapeDtypeStruct((m, n), F32) for _ in range(n_out)]
    out_specs = [pl.BlockSpec((TM, tn), lambda j, i: (i, j)) for _ in range(n_out)]
    return pl.pallas_call(
        functools.partial(kernel_fn, n_a=len(a_parts), npb=mp // TM),
        out_shape=out_shape if n_out > 1 else out_shape[0],
        grid=(nj, m // TM),
        in_specs=_a_specs(a_parts, mp // TM) + [_w_spec(k, tn, layer, s * nj) for s in range(n_split)],
        out_specs=out_specs if n_out > 1 else out_specs[0],
        scratch_shapes=[pltpu.VMEM((n_split, k, tn), BF16)],
        compiler_params=_cparams(("parallel", "arbitrary")),
        name=name,
    )(*a_parts, *([w] * n_split))


def _mm_resid(a_parts, w, layer, x, ap, as_, gate_idx, mp, lp, glu=False):
    m = x.shape[0]
    k = a_parts[0].shape[1]
    n_split = 2 if glu else 1
    n = w.shape[2] // n_split
    tn = 512 if glu else 1024
    nj = n // tn
    bs = as_.shape[0]
    return pl.pallas_call(
        functools.partial(_mm_resid_kernel, n_a=len(a_parts), npb=mp // TM, bpb=lp // TM, glu=glu),
        out_shape=jax.ShapeDtypeStruct((m, n), F32),
        grid=(nj, m // TM),
        in_specs=_a_specs(a_parts, mp // TM)
        + [_w_spec(k, tn, layer, s * nj) for s in range(n_split)]
        + [
            pl.BlockSpec((TM, tn), lambda j, i: (i, j)),
            pl.BlockSpec((8, tn), lambda j, i: (0, gate_idx * nj + j)),
            pl.BlockSpec((bs, tn), lambda j, i: (0, gate_idx * nj + j)),
        ],
        out_specs=pl.BlockSpec((TM, tn), lambda j, i: (i, j)),
        scratch_shapes=[pltpu.VMEM((n_split, k, tn), BF16)],
        compiler_params=_cparams(("parallel", "arbitrary")),
        name="mm_glu_resid" if glu else "mm_resid",
    )(*a_parts, *([w] * n_split), x, ap, as_)


S5_LC = 512


def _s5_scan(bu_ref, st_ref, are_ref, aim_ref, nb, tc):
    n_tiles = nb // 8

    def tile(r, carry):
        r0 = pl.multiple_of(r * 8, 8)
        for c in range(SSM_ST // S5_LC):
            cre = pl.ds(c * S5_LC, S5_LC)
            cim = pl.ds(SSM_ST + c * S5_LC, S5_LC)
            ar = are_ref[:, cre]
            ai = aim_ref[:, cre]
            srow = pl.ds(r0, 8)
            hr = st_ref[srow, cre]
            hi = st_ref[srow, cim]

            def step(t, carry):
                hr, hi = carry
                row = pl.ds(pl.multiple_of(t * nb + r0, 8), 8)
                nr = ar * hr - ai * hi + bu_ref[row, cre]
                ni = ar * hi + ai * hr + bu_ref[row, cim]
                bu_ref[row, cre] = nr
                bu_ref[row, cim] = ni
                return nr, ni

            hr, hi = lax.fori_loop(0, tc, step, (hr, hi), unroll=8)
            st_ref[srow, cre] = hr
            st_ref[srow, cim] = hi
        return carry

    lax.fori_loop(0, n_tiles, tile, 0)


def _s5_core(u, bu_ref, st_ref, h0re_ref, h0im_ref, are_ref, aim_ref, bw_ref, cw_ref, dsk_ref,
             sre_ref, sim_ref, scan):
    t = pl.program_id(1)

    @pl.when(t == 0)
    def _():
        for gl in range(SSM_GB):
            st_ref[:, gl * SSM_STATE:(gl + 1) * SSM_STATE] = h0re_ref[:, gl, :]
            st_ref[:, SSM_ST + gl * SSM_STATE:SSM_ST + (gl + 1) * SSM_STATE] = h0im_ref[:, gl, :]

    bu_ref[...] = _dotb(u, bw_ref[...])
    scan(bu_ref, st_ref, are_ref, aim_ref)
    y = _dot(bu_ref[...].astype(BF16), cw_ref[...].astype(BF16)) + dsk_ref[...] * u

    @pl.when(t == pl.num_programs(1) - 1)
    def _():
        for gl in range(SSM_GB):
            sre_ref[:, gl, :] = st_ref[:, gl * SSM_STATE:(gl + 1) * SSM_STATE]
            sim_ref[:, gl, :] = st_ref[:, SSM_ST + gl * SSM_STATE:SSM_ST + (gl + 1) * SSM_STATE]

    return _gelu_tanh(y)


S5_NSUB = 4


def _s5_scan_pairs_sub(bu_ref, h_ref, ar, ai, carry):
    n_lc = len(ar)
    first_half = lax.broadcasted_iota(I32, (8, S5_LC), 0) < 4
    for j in range(bu_ref.shape[0] // 8):
        rows = slice(j * 8, j * 8 + 8)
        new = []
        for c in range(n_lc):
            cre = slice(c * S5_LC, (c + 1) * S5_LC)
            cim = slice(SSM_ST + c * S5_LC, SSM_ST + (c + 1) * S5_LC)
            hr, hi = carry[c]
            br = bu_ref[rows, cre]
            bi = bu_ref[rows, cim]
            er = ar[c] * hr - ai[c] * hi + br
            ei = ar[c] * hi + ai[c] * hr + bi
            sr = pltpu.roll(er, 4, 0)
            si = pltpu.roll(ei, 4, 0)
            orr = ar[c] * sr - ai[c] * si + br
            oi = ar[c] * si + ai[c] * sr + bi
            h_ref[rows, cre] = jnp.where(first_half, er, orr)
            h_ref[rows, cim] = jnp.where(first_half, ei, oi)
            new.append((pltpu.roll(orr, 4, 0), pltpu.roll(oi, 4, 0)))
        carry = new
    return carry


def _s5_prompt_kernel(*refs, nbatch, tc):
    u_refs = refs[:nbatch]
    (h0re_ref, h0im_ref, are_ref, aim_ref, bw_ref, cw_ref, dsk_ref,
     p_ref, sre_ref, sim_ref, r_ref, st_ref, y_ref) = refs[nbatch:nbatch + 13]
    bu_refs = refs[nbatch + 13:nbatch + 13 + S5_NSUB]
    h_refs = refs[nbatch + 13 + S5_NSUB:]
    t = pl.program_id(1)

    @pl.when(t == 0)
    def _():
        st_ref[:, 0:SSM_ST] = h0re_ref[...]
        st_ref[:, SSM_ST:2 * SSM_ST] = h0im_ref[...]

    nslab = r_ref.shape[0]
    for b in range(nbatch):
        ub = u_refs[b][...]
        for s in range(nslab):
            r_ref[s, pl.ds(b, tc, stride=nbatch), :] = ub[:, s * 128:(s + 1) * 128]
    u = jnp.concatenate([r_ref[s] for s in range(nslab)], axis=1)
    u_bf = u.astype(BF16)
    bw = bw_ref[...].astype(BF16)
    cw = cw_ref[...].astype(BF16)
    sub = u.shape[0] // S5_NSUB
    n_lc = SSM_ST // S5_LC
    ar = [are_ref[:, c * S5_LC:(c + 1) * S5_LC] for c in range(n_lc)]
    ai = [aim_ref[:, c * S5_LC:(c + 1) * S5_LC] for c in range(n_lc)]
    carry = [(st_ref[:, c * S5_LC:(c + 1) * S5_LC], st_ref[:, SSM_ST + c * S5_LC:SSM_ST + (c + 1) * S5_LC])
             for c in range(n_lc)]

    def b_proj(s):
        bu_refs[s][...] = _dot(u_bf[s * sub:(s + 1) * sub], bw)

    def c_proj(s):
        return _dot(h_refs[s][...].astype(BF16), cw)

    ys = []
    b_proj(0)
    for s in range(S5_NSUB):
        if s + 1 < S5_NSUB:
            b_proj(s + 1)
        carry = _s5_scan_pairs_sub(bu_refs[s], h_refs[s], ar, ai, carry)
        if s >= 1:
            ys.append(c_proj(s - 1))
    ys.append(c_proj(S5_NSUB - 1))
    for c in range(n_lc):
        st_ref[:, c * S5_LC:(c + 1) * S5_LC] = carry[c][0]
        st_ref[:, SSM_ST + c * S5_LC:SSM_ST + (c + 1) * S5_LC] = carry[c][1]
    y = _gelu_tanh(jnp.concatenate(ys, axis=0) + dsk_ref[...] * u)
    for s in range(nslab):
        y_ref[s] = y[:, s * 128:(s + 1) * 128]
    for b in range(nbatch):
        p_ref[b] = jnp.concatenate([y_ref[s, pl.ds(b, tc, stride=nbatch), :] for s in range(nslab)],
                                   axis=1).astype(p_ref.dtype)

    @pl.when(t == pl.num_programs(1) - 1)
    def _():
        sre_ref[...] = st_ref[:, 0:SSM_ST]
        sim_ref[...] = st_ref[:, SSM_ST:2 * SSM_ST]


def _s5_sample_kernel(u_ref, h0re_ref, h0im_ref, are_ref, aim_ref, bw_ref, cw_ref, dsk_ref,
                      p_ref, sre_ref, sim_ref, bu_ref, st_ref, *, nb, tc):
    scan = functools.partial(_s5_scan, nb=nb, tc=tc)
    p_ref[...] = _s5_core(u_ref[...], bu_ref, st_ref, h0re_ref, h0im_ref, are_ref, aim_ref, bw_ref, cw_ref,
                          dsk_ref, sre_ref, sim_ref, scan).astype(p_ref.dtype)


def _s5_params(a_re, a_im, log_dt, b_re, b_im, c_re, c_im):
    g, p = a_re.shape
    dt = jnp.exp(log_dt)[:, None]
    mag = jnp.exp(dt * a_re)
    ang = dt * a_im
    ab_re = mag * jnp.cos(ang)
    ab_im = mag * jnp.sin(ang)
    n_re = ab_re - 1.0
    n_im = ab_im
    den = a_re * a_re + a_im * a_im
    f_re = ((n_re * a_re + n_im * a_im) / den)[..., None]
    f_im = ((n_im * a_re - n_re * a_im) / den)[..., None]
    bb_re = f_re * b_re - f_im * b_im
    bb_im = f_re * b_im + f_im * b_re
    ngb = g // SSM_GB
    eye = jnp.eye(SSM_GB, dtype=F32)

    def bdiag_b(bb):
        t = bb.reshape(ngb, SSM_GB, p, SSM_GROUP).transpose(0, 1, 3, 2)
        t = t[:, :, :, None, :] * eye[None, :, None, :, None]
        return t.reshape(ngb, SSM_CH, SSM_ST).astype(BF16)

    def bdiag_c(cc):
        t = cc.reshape(ngb, SSM_GB, SSM_GROUP, p).transpose(0, 1, 3, 2)
        t = t[:, :, :, None, :] * eye[None, :, None, :, None]
        return t.reshape(ngb, SSM_ST, SSM_CH).astype(BF16)

    bw =jnp.concatenate([bdiag_b(bb_re), bdiag_b(bb_im)], axis=2)
    cw = jnp.concatenate([bdiag_c(c_re), -bdiag_c(c_im)], axis=1)
    are = jnp.broadcast_to(ab_re.reshape(1, g * p), (8, g * p))
    aim = jnp.broadcast_to(ab_im.reshape(1, g * p), (8, g * p))
    return are, aim, bw, cw


def _s5_common_specs(nb):
    return [
        pl.BlockSpec((nb, SSM_ST), lambda gb, t: (0, gb)),
        pl.BlockSpec((nb, SSM_ST), lambda gb, t: (0, gb)),
        pl.BlockSpec((8, SSM_ST), lambda gb, t: (0, gb)),
        pl.BlockSpec((8, SSM_ST), lambda gb, t: (0, gb)),
        pl.BlockSpec((None, SSM_CH, 2 * SSM_ST), lambda gb, t: (gb, 0, 0)),
        pl.BlockSpec((None, 2 * SSM_ST, SSM_CH), lambda gb, t: (gb, 0, 0)),
        pl.BlockSpec((1, SSM_CH), lambda gb, t: (0, gb)),
    ]


def _s5_prompt(h, nbatch, lp, params, dsk):
    assert nbatch == 4, "the paired scan packs two steps of 4 sequences into one 8-row tile"
    are, aim, bw, cw = params
    d = h.shape[1]
    ngb = d // SSM_CH
    tc = 256
    nt = lp // tc
    gp = are.shape[1]
    zeros = jnp.zeros((8, gp), F32)
    u_specs = [pl.BlockSpec((tc, SSM_CH), functools.partial(lambda gb, t, b: (b * nt + t, gb), b=b))
               for b in range(nbatch)]
    rows = tc * nbatch
    return pl.pallas_call(
        functools.partial(_s5_prompt_kernel, nbatch=nbatch, tc=tc),
        out_shape=(jax.ShapeDtypeStruct((nbatch, lp, d), BF16),
                   jax.ShapeDtypeStruct((8, gp), F32), jax.ShapeDtypeStruct((8, gp), F32)),
        grid=(ngb, nt),
        in_specs=u_specs + _s5_common_specs(8),
        out_specs=(pl.BlockSpec((nbatch, tc, SSM_CH), lambda gb, t: (0, t, gb)),
                   pl.BlockSpec((8, SSM_ST), lambda gb, t: (0, gb)),
                   pl.BlockSpec((8, SSM_ST), lambda gb, t: (0, gb))),
        scratch_shapes=[pltpu.VMEM((SSM_CH // 128, rows, 128), F32), pltpu.VMEM((8, 2 * SSM_ST), F32),
                        pltpu.VMEM((SSM_CH // 128, rows, 128), F32)]
        + [pltpu.VMEM((rows // S5_NSUB, 2 * SSM_ST), F32)] * (2 * S5_NSUB),
        compiler_params=_cparams(("parallel", "arbitrary")),
        name="s5_prompt",
    )(*([h] * nbatch), zeros, zeros, are, aim, bw, cw, dsk)


def _s5_sample(h, mp, bs, ls, h0re, h0im, params, dsk):
    are, aim, bw, cw = params
    d = h.shape[1]
    ngb = d // SSM_CH
    rows = ls * bs
    blk0 = mp // rows
    g, p = h0re.shape[1:]
    state_spec = pl.BlockSpec((bs, SSM_GB, p), lambda gb, t: (0, gb, 0))
    return pl.pallas_call(
        functools.partial(_s5_sample_kernel, nb=bs, tc=ls),
        out_shape=(jax.ShapeDtypeStruct((rows, d), BF16),
                   jax.ShapeDtypeStruct((bs, g, p), F32), jax.ShapeDtypeStruct((bs, g, p), F32)),
        grid=(ngb, 1),
        in_specs=[pl.BlockSpec((rows, SSM_CH), lambda gb, t: (blk0, gb)), state_spec, state_spec]
        + _s5_common_specs(bs)[2:],
        out_specs=(pl.BlockSpec((rows, SSM_CH), lambda gb, t: (0, gb)), state_spec, state_spec),
        scratch_shapes=[pltpu.VMEM((rows, 2 * SSM_ST), F32), pltpu.VMEM((bs, 2 * SSM_ST), F32)],
        compiler_params=_cparams(("parallel", "arbitrary")),
        name="s5_sample",
    )(h, h0re, h0im, are, aim, bw, cw, dsk)


def _ln_silu(y, g, b):
    mu = jnp.mean(y, axis=-1, keepdims=True)
    yc = y - mu
    z = yc * lax.rsqrt(jnp.mean(yc * yc, axis=-1, keepdims=True) + EPS) * g + b
    return _silu(z)


def _conv_prompt_kernel(cur_ref, halo_ref, buf_ref, w_ref, *rest, width, mode):
    if mode == "cf":
        bdw_ref, lng_ref, lnb_ref, o_ref, s_ref, acc_ref = rest
    else:
        bg_ref, o_ref, s_ref, acc_ref = rest
    t = pl.program_id(1)
    hb = halo_ref.shape[0]
    tc, d = cur_ref.shape
    first = t == 0
    s_ref[0:hb, :] = jnp.where(first, buf_ref[0], halo_ref[...])
    s_ref[hb:hb + tc, :] = cur_ref[...]
    s_ref[hb + tc:hb + tc + 8, :] = jnp.zeros((8, d), F32)
    off = hb - (width - 1)
    by_shift = [[(o // 8, o - off) for o in range(off, off + width) if o % 8 == sh] for sh in range(8)]
    rc, lch = 64, 256
    for r0 in range(0, tc, rc):
        for c in range(d // lch):
            cols = pl.ds(c * lch, lch)
            out = None
            for sh, taps in enumerate(by_shift):
                z = None
                for q, k in taps:
                    t = w_ref[k:k + 1, cols] * s_ref[r0 + 8 * q:r0 + 8 * q + rc + 8, cols]
                    z = t if z is None else z + t
                if z is not None:
                    zs = z[sh:sh + rc]
                    out = zs if out is None else out + zs
            acc_ref[r0:r0 + rc, cols] = out
    if mode == "cf":
        o_ref[...] = _ln_silu(acc_ref[...] + bdw_ref[...], lng_ref[...], lnb_ref[...]).astype(o_ref.dtype)
    else:
        o_ref[...] = (bg_ref[...] * acc_ref[...]).astype(o_ref.dtype)


def _conv_prompt(v, nbatch, lp, w, layer, mode, extra):
    d = v.shape[1]
    width = w.shape[1]
    hb = 32 if width > 9 else 8
    tc = 256
    nt = lp // tc
    zeros = jnp.zeros((nbatch, hb, d), F32)
    in_specs = [
        pl.BlockSpec((tc, d), lambda b, t: (b * nt + t, 0)),
        pl.BlockSpec((hb, d), lambda b, t: (jnp.maximum((b * lp + t * tc) // hb - 1, 0), 0)),
        pl.BlockSpec((1, hb, d), lambda b, t: (b, 0, 0)),
        pl.BlockSpec((None, width, d), lambda b, t: (layer, 0, 0)),
    ]
    if mode == "cf":
        in_specs += [pl.BlockSpec((1, d), lambda b, t: (layer, 0))] * 3
    else:
        in_specs += [pl.BlockSpec((tc, d), lambda b, t: (b * nt + t, 0))]
    return pl.pallas_call(
        functools.partial(_conv_prompt_kernel, width=width, mode=mode),
        out_shape=jax.ShapeDtypeStruct((nbatch * lp, d), BF16),
        grid=(nbatch, nt),
        in_specs=in_specs,
        out_specs=pl.BlockSpec((tc, d), lambda b, t: (b * nt + t, 0)),
        scratch_shapes=[pltpu.VMEM((hb + tc + 8, d), F32), pltpu.VMEM((tc, d), F32)],
        compiler_params=_cparams(("parallel", "arbitrary")),
        name="conv_prompt_" + mode,
    )(v, v, zeros, w, *extra)


def _conv_sample_kernel(v_ref, cache_ref, w_ref, *rest, width, mode):
    if mode == "cf":
        bdw_ref, lng_ref, lnb_ref, o_ref = rest
    else:
        bg_ref, o_ref = rest
    ls = v_ref.shape[0]
    hist = width - 1

    def full(j):
        return cache_ref[j] if j < hist else v_ref[j - hist]

    for l in range(ls):
        acc = w_ref[0:1, :] * full(l)
        for k in range(1, width):
            acc = acc + w_ref[k:k + 1, :] * full(l + k)
        if mode == "cf":
            o_ref[l] = _ln_silu(acc + bdw_ref[...], lng_ref[...], lnb_ref[...]).astype(o_ref.dtype)
        else:
            o_ref[l] = (bg_ref[l] * acc).astype(o_ref.dtype)


def _conv_sample(v_tm, cache_tm, w, layer, mode, extra):
    ls, bs, d = v_tm.shape
    width = w.shape[1]
    bc = 16
    in_specs = [
        pl.BlockSpec((ls, bc, d), lambda i: (0, i, 0)),
        pl.BlockSpec((width - 1, bc, d), lambda i: (0, i, 0)),
        pl.BlockSpec((None, width, d), lambda i: (layer, 0, 0)),
    ]
    if mode == "cf":
        in_specs += [pl.BlockSpec((1, d), lambda i: (layer, 0))] * 3
    else:
        in_specs += [pl.BlockSpec((ls, bc, d), lambda i: (0, i, 0))]
    return pl.pallas_call(
        functools.partial(_conv_sample_kernel, width=width, mode=mode),
        out_shape=jax.ShapeDtypeStruct((ls, bs, d), BF16),
        grid=(bs // bc,),
        in_specs=in_specs,
        out_specs=pl.BlockSpec((ls, bc, d), lambda i: (0, i, 0)),
        compiler_params=_cparams(("parallel",)),
        name="conv_sample_" + mode,
    )(v_tm, cache_tm, w, *extra)


def _onehot2(r):
    lane = lax.broadcasted_iota(I32, r.shape, 1).astype(F32)
    return jnp.where((lane == r[:, 0:1]) | (lane == r[:, 1:2]), 1.0, 0.0)


def _moe_plan_kernel(r_ref, pos_ref, info_ref, cnt_ref, run_ref, pst_ref):
    ph = pl.program_id(0)
    i = pl.program_id(1)
    r = r_ref[...]
    tb = r.shape[0]
    oh = _onehot2(r)
    colsum = jnp.broadcast_to(jnp.sum(oh, axis=0, keepdims=True), (8, ROUTE_LANES))

    @pl.when((ph == 0) & (i == 0))
    def _():
        cnt_ref[...] = jnp.zeros_like(cnt_ref)

    @pl.when(ph == 0)
    def _():
        cnt_ref[...] += colsum

    @pl.when((ph == 1) & (i == 0))
    def _():
        cnt = cnt_ref[...]
        nblk = jnp.floor((cnt + (MOE_BM - 1)) * (1.0 / MOE_BM))
        li = lax.broadcasted_iota(I32, (ROUTE_LANES, ROUTE_LANES), 0)
        lj = lax.broadcasted_iota(I32, (ROUTE_LANES, ROUTE_LANES), 1)
        upper = jnp.where(li <= lj, 1.0, 0.0).astype(BF16)
        bend = _dot(nblk.astype(BF16), upper)
        pst_ref[...] = (bend - nblk) * MOE_BM
        run_ref[...] = jnp.zeros_like(run_ref)
        n_used = bend[0:1, N_EXPERTS - 1:N_EXPERTS]
        bi = lax.broadcasted_iota(I32, (PLAN_ROWS, ROUTE_LANES), 0).astype(F32)
        ln = lax.broadcasted_iota(I32, (PLAN_ROWS, ROUTE_LANES), 1)
        ln_f = ln.astype(F32)
        pe = jnp.broadcast_to(bend[0:1, :], (PLAN_ROWS, ROUTE_LANES))
        pk = jnp.broadcast_to(nblk[0:1, :], (PLAN_ROWS, ROUTE_LANES))
        bcl = jnp.minimum(bi, n_used - 1.0)
        blk_e = jnp.sum(jnp.where((ln < N_EXPERTS) & (pe <= bcl), 1.0, 0.0), axis=1, keepdims=True)
        bend_col = jnp.sum(jnp.where(ln_f == bi, pe, 0.0), axis=1, keepdims=True)
        nblk_col = jnp.sum(jnp.where(ln_f == bi, pk, 0.0), axis=1, keepdims=True)
        info = jnp.where(ln == 0, blk_e, 0.0)
        info = jnp.where(ln == 1, n_used, info)
        info = jnp.where(ln == 2, bend_col, info)
        info = jnp.where(ln == 3, nblk_col, info)
        nonempty = (ln < N_EXPERTS) & (pk > 0.0)
        seq = jnp.sum(jnp.where(nonempty & (ln_f < blk_e), 1.0, 0.0), axis=1, keepdims=True)
        nxt = jnp.min(jnp.where(nonempty & (ln_f > blk_e), ln_f, float(ROUTE_LANES)), axis=1, keepdims=True)
        info = jnp.where(ln == 4, seq, info)
        info = jnp.where(ln == 5, nxt, info)
        info_ref[...] = info.astype(I32)

    @pl.when(ph == 1)
    def _():
        ti = lax.broadcasted_iota(I32, (tb, tb), 0)
        tj = lax.broadcasted_iota(I32, (tb, tb), 1)
        lower = jnp.where(ti > tj, 1.0, 0.0).astype(BF16)
        before = _dot(lower, oh.astype(BF16))
        val = pst_ref[0:1, :] + run_ref[0:1, :] + before
        lane = lax.broadcasted_iota(I32, r.shape, 1)
        lane_f = lane.astype(F32)
        p0 = jnp.sum(jnp.where(lane_f == r[:, 0:1], val, 0.0), axis=1, keepdims=True)
        p1 = jnp.sum(jnp.where(lane_f == r[:, 1:2], val, 0.0), axis=1, keepdims=True)
        pos_ref[...] = jnp.where(lane == 0, p0, jnp.where(lane == 1, p1, 0.0)).astype(I32)
        run_ref[...] += colsum


def _moe_plan(route):
    m = route.shape[0]
    nb = m // PLAN_TB
    return pl.pallas_call(
        _moe_plan_kernel,
        out_shape=(jax.ShapeDtypeStruct((m, ROUTE_LANES), I32), jax.ShapeDtypeStruct((PLAN_ROWS, ROUTE_LANES), I32)),
        grid=(2, nb),
        in_specs=[pl.BlockSpec((PLAN_TB, ROUTE_LANES), lambda ph, i: (i, 0))],
        out_specs=(pl.BlockSpec((PLAN_TB, ROUTE_LANES), lambda ph, i: (ph * i, 0)),
                   pl.BlockSpec((PLAN_ROWS, ROUTE_LANES), lambda ph, i: (0, 0))),
        scratch_shapes=[pltpu.VMEM((8, ROUTE_LANES), F32)] * 3,
        compiler_params=_cparams(("arbitrary", "arbitrary")),
        name="moe_plan",
    )(route)


def _moe_scatter_kernel(pos_ref, bend_ref, nblk_ref, nu_ref, h_ref, xs_ref, z_ref, sem, zsem):
    i = pl.program_id(0)
    n_blocks = xs_ref.shape[0] // MOE_BM
    rows = h_ref.shape[0]

    @pl.when(i == 0)
    def _():
        z_ref[...] = jnp.zeros_like(z_ref)

        def zero_copy(blk):
            start = pl.multiple_of(blk * MOE_BM, MOE_BM)
            return pltpu.make_async_copy(z_ref, xs_ref.at[pl.ds(start, MOE_BM)], zsem)

        def for_each_pad_block(fn):
            def per_expert(e, c):
                @pl.when(nblk_ref[e] > 0)
                def _():
                    fn(zero_copy(bend_ref[e] - 1))
                return c

            def per_tail_block(blk, c):
                @pl.when(blk >= nu_ref[0])
                def _():
                    fn(zero_copy(blk))
                return c

            lax.fori_loop(0, N_EXPERTS, per_expert, 0)
            lax.fori_loop(0, n_blocks, per_tail_block, 0)

        for_each_pad_block(lambda c: c.start())
        for_each_pad_block(lambda c: c.wait())

    def row(r, c):
        base = (i * rows + r) * 2
        for k in range(2):
            p = pos_ref[base + k]
            pltpu.make_async_copy(h_ref.at[pl.ds(r, 1)], xs_ref.at[pl.ds(p, 1)], sem).start()
        return c

    lax.fori_loop(0, rows, row, 0, unroll=4)
    for k in range(2):
        pltpu.make_async_copy(h_ref, xs_ref.at[pl.ds(0, rows)], sem).wait()


def _moe_scatter(h, pos2, bend, nblk, n_used, n_rows):
    m, d = h.shape
    grid_spec = pltpu.PrefetchScalarGridSpec(
        num_scalar_prefetch=4,
        grid=(m // TMS,),
        in_specs=[pl.BlockSpec((TMS, d), lambda i, *_: (i, 0))],
        out_specs=pl.BlockSpec(memory_space=pl.ANY),
        scratch_shapes=[pltpu.VMEM((MOE_BM, d), F32), pltpu.SemaphoreType.DMA(()), pltpu.SemaphoreType.DMA(())],
    )
    return pl.pallas_call(
        _moe_scatter_kernel,
        out_shape=jax.ShapeDtypeStruct((n_rows, d), F32),
        grid_spec=grid_spec,
        compiler_params=_cparams(("arbitrary",)),
        name="moe_scatter",
    )(pos2, bend, nblk, n_used, h)


def _moe_expert_kernel(be_ref, nu_ref, seq_ref, nxt_ref, xs_ref, w13_hbm, w2_hbm, y_ref,
                       w13f_ref, w2f_ref, w13b_ref, w2b_ref, sem13, sem2, *, layer):
    i = pl.program_id(0)
    used = i < nu_ref[0]
    e = be_ref[i]
    new_expert = (i == 0) | (e != be_ref[jnp.maximum(i - 1, 0)])
    slot = seq_ref[i] % 2

    def fetch(expert, s):
        return (pltpu.make_async_copy(w13_hbm.at[layer, expert], w13f_ref.at[s], sem13.at[s]),
                pltpu.make_async_copy(w2_hbm.at[layer, expert], w2f_ref.at[s], sem2.at[s]))

    @pl.when(i == 0)
    def _():
        for c in fetch(e, slot):
            c.start()

    @pl.when(used & new_expert)
    def _():
        nxt = nxt_ref[i]

        @pl.when(nxt < N_EXPERTS)
        def _():
            for c in fetch(nxt, 1 - slot):
                c.start()

        for c in fetch(e, slot):
            c.wait()
        w13b_ref[...] = w13f_ref[slot].astype(BF16)
        w2b_ref[...] = w2f_ref[slot].astype(BF16)

    @pl.when(used)
    def _():
        ab = _dot(xs_ref[...].astype(BF16), w13b_ref[...])
        a = ab[:, 0:D_EXPERT]
        b = ab[:, D_EXPERT:2 * D_EXPERT]
        y_ref[...] = _dot((_silu(a) * b).astype(BF16), w2b_ref[...])

    @pl.when(jnp.logical_not(used))
    def _():
        y_ref[...] = jnp.zeros_like(y_ref)


def _moe_experts(xs, blk_e, n_used, seq, nxt, w13, w2, layer):
    n_rows, d = xs.shape
    n_blocks = blk_e.shape[0]
    de2 = w13.shape[3]
    grid_spec = pltpu.PrefetchScalarGridSpec(
        num_scalar_prefetch=4,
        grid=(n_blocks,),
        in_specs=[
            pl.BlockSpec((MOE_BM, d), lambda i, be, nu, sq, nx: (jnp.minimum(i, nu[0] - 1), 0)),
            pl.BlockSpec(memory_space=pl.ANY),
            pl.BlockSpec(memory_space=pl.ANY),
        ],
        out_specs=pl.BlockSpec((MOE_BM, d), lambda i, be, nu, sq, nx: (i, 0)),
        scratch_shapes=[pltpu.VMEM((2, d, de2), F32), pltpu.VMEM((2, de2 // 2, d), F32),
                        pltpu.VMEM((d, de2), BF16), pltpu.VMEM((de2 // 2, d), BF16),
                        pltpu.SemaphoreType.DMA((2,)), pltpu.SemaphoreType.DMA((2,))],
    )
    return pl.pallas_call(
        functools.partial(_moe_expert_kernel, layer=layer),
        out_shape=jax.ShapeDtypeStruct((n_rows, d), F32),
        grid_spec=grid_spec,
        compiler_params=_cparams(("arbitrary",)),
        name="moe_experts",
    )(blk_e, n_used, seq, nxt, xs, w13, w2)


def _moe_combine_rows(pos_ref, y_ref, x_ref, r_ref, gp_ref, gs_ref, ybuf, sem, npb, bpb):
    i = pl.program_id(0)
    n = pl.num_programs(0)
    slot = i % 2

    def gather(blk, s):
        def row(r, c):
            base = (blk * TMC + r) * 2
            for k in range(2):
                p = pos_ref[base + k]
                pltpu.make_async_copy(y_ref.at[pl.ds(p, 1)], ybuf.at[s, k, pl.ds(r, 1)], sem.at[s]).start()
            return c
        lax.fori_loop(0, TMC, row, 0, unroll=4)

    @pl.when(i == 0)
    def _():
        gather(0, 0)

    @pl.when(i + 1 < n)
    def _():
        gather(i + 1, 1 - slot)

    for k in range(2):
        pltpu.make_async_copy(y_ref.at[pl.ds(0, TMC)], ybuf.at[slot, k], sem.at[slot]).wait()

    r = r_ref[...]
    f = r[:, 2:3] * ybuf[slot, 0] + r[:, 3:4] * ybuf[slot, 1]
    gate = _seq_vec(gp_ref, gs_ref, i, npb, bpb, TMC)
    return x_ref[...] + gate * f


def _moe_combine_kernel(pos_ref, y_ref, x_ref, r_ref, gp_ref, gs_ref, o_ref, ybuf, sem, *, npb, bpb):
    o_ref[...] = _moe_combine_rows(pos_ref, y_ref, x_ref, r_ref, gp_ref, gs_ref, ybuf, sem, npb, bpb)


def _moe_combine_kernel_fused(pos_ref, y_ref, x_ref, r_ref, gp_ref, gs_ref, g_ref, shp_ref, scp_ref, shs_ref,
                              scs_ref, o_ref, h_ref, ybuf, sem, *, npb, bpb):
    i = pl.program_id(0)
    x_new = _moe_combine_rows(pos_ref, y_ref, x_ref, r_ref, gp_ref, gs_ref, ybuf, sem, npb, bpb)
    o_ref[...] = x_new
    sc = _seq_vec(scp_ref, scs_ref, i, npb, bpb, TMC)
    sh = _seq_vec(shp_ref, shs_ref, i, npb, bpb, TMC)
    h_ref[...] = (_rms(x_new, g_ref[...]) * (1.0 + sc) + sh).astype(h_ref.dtype)


def _moe_combine_kernel_final(pos_ref, y_ref, x_ref, r_ref, gp_ref, gs_ref, g_ref, op_ref, os_ref, ybuf, sem,
                              *, npb, bpb):
    i = pl.program_id(0)
    x_new = _moe_combine_rows(pos_ref, y_ref, x_ref, r_ref, gp_ref, gs_ref, ybuf, sem, npb, bpb)
    out = _rms(x_new, g_ref[...])

    @pl.when(i < npb)
    def _():
        op_ref[...] = out

    @pl.when(i >= npb)
    def _():
        os_ref[...] = out


def _moe_combine(y, pos2, x, route, ap, as_, gate_idx, mp, lp, nxt=None, final_g=None):
    m, d = x.shape
    bs = as_.shape[0]
    npb = mp // TMC
    in_specs = [
        pl.BlockSpec(memory_space=pl.ANY),
        pl.BlockSpec((TMC, d), lambda i, pos: (i, 0)),
        pl.BlockSpec((TMC, ROUTE_LANES), lambda i, pos: (i, 0)),
        pl.BlockSpec((8, d), lambda i, pos: (0, gate_idx)),
        pl.BlockSpec((bs, d), lambda i, pos: (0, gate_idx)),
    ]
    out_shape = jax.ShapeDtypeStruct((m, d), F32)
    out_specs = pl.BlockSpec((TMC, d), lambda i, pos: (i, 0))
    args = (pos2, y, x, route, ap, as_)
    kernel_fn = _moe_combine_kernel
    if nxt is not None:
        g_n, ap_n, as_n, dtype_n = nxt
        in_specs += [
            pl.BlockSpec((1, d), lambda i, pos: (0, 0)),
            pl.BlockSpec((8, d), lambda i, pos: (0, 0)),
            pl.BlockSpec((8, d), lambda i, pos: (0, 1)),
            pl.BlockSpec((bs, d), lambda i, pos: (0, 0)),
            pl.BlockSpec((bs, d), lambda i, pos: (0, 1)),
        ]
        out_shape = (out_shape, jax.ShapeDtypeStruct((m, d), dtype_n))
        out_specs = (out_specs, pl.BlockSpec((TMC, d), lambda i, pos: (i, 0)))
        args += (g_n, ap_n, ap_n, as_n, as_n)
        kernel_fn = _moe_combine_kernel_fused
    elif final_g is not None:
        in_specs += [pl.BlockSpec((1, d), lambda i, pos: (0, 0))]
        out_shape = (jax.ShapeDtypeStruct((mp, d), F32), jax.ShapeDtypeStruct((m - mp, d), F32))
        out_specs = (pl.BlockSpec((TMC, d), lambda i, pos: (jnp.minimum(i, npb - 1), 0)),
                     pl.BlockSpec((TMC, d), lambda i, pos: (jnp.maximum(i - npb, 0), 0)))
        args += (final_g,)
        kernel_fn = _moe_combine_kernel_final
    grid_spec = pltpu.PrefetchScalarGridSpec(
        num_scalar_prefetch=1,
        grid=(m // TMC,),
        in_specs=in_specs,
        out_specs=out_specs,
        scratch_shapes=[pltpu.VMEM((2, 2, TMC, d), F32), pltpu.SemaphoreType.DMA((2,))],
    )
    return pl.pallas_call(
        functools.partial(kernel_fn, npb=npb, bpb=lp // TMC),
        out_shape=out_shape,
        grid_spec=grid_spec,
        compiler_params=_cparams(("arbitrary",)),
        name="moe_combine",
    )(*args)


def _moe_layer(x, h2, route, w13, w2, layer, ap, as_, mp, lp, nxt=None, final_g=None):
    m = x.shape[0]
    n_blocks = (2 * m + N_EXPERTS * (MOE_BM - 1) + MOE_BM - 1) // MOE_BM
    assert n_blocks <= PLAN_ROWS and m % PLAN_TB == 0 and m % TMC == 0 and m % TMS == 0
    pos, info = _moe_plan(route)
    pos2 = pos[:, 0:2].reshape(2 * m)
    blk_e, n_used = info[:n_blocks, 0], info[0:1, 1]
    bend, nblk = info[:N_EXPERTS, 2], info[:N_EXPERTS, 3]
    seq, nxt_e = info[:n_blocks, 4], info[:n_blocks, 5]
    xs = _moe_scatter(h2, pos2, bend, nblk, n_used, n_blocks * MOE_BM)
    y = _moe_experts(xs, blk_e, n_used, seq, nxt_e, w13, w2, layer)
    return _moe_combine(y, pos2, x, route, ap, as_, 5, mp, lp, nxt=nxt, final_g=final_g)


def kernel(x_prompt, x_sample, c_prompt, c_sample, state_l0_ssm_re, state_l0_ssm_im, cache_l1_conformer_conv, cache_l2_short_conv, state_l3_ssm_re, state_l3_ssm_im, norm1_g, norm2_g, w_ada, b_ada, final_norm_g, ssm_a_re, ssm_a_im, ssm_log_dt, ssm_b_re, ssm_b_im, ssm_c_re, ssm_c_im, ssm_d, ssm_w_glu, cf_w_pw1, cf_w_dw, cf_b_dw, cf_ln_g, cf_ln_b, cf_w_pw2, sc_w_in, sc_w_conv, sc_w_out, moe_w_group, moe_b_group, moe_w_expert, moe_b_expert, moe_w13, moe_w2):
    bp, lp, d = x_prompt.shape
    bs, ls, _ = x_sample.shape
    depth = w_ada.shape[0]
    mp, ms = bp * lp, bs * ls
    g_ssm, p_ssm = ssm_a_re.shape[1:]
    assert d == D_MODEL and lp % TM == 0 and ms % TM == 0 and TM % bs == 0 and bp <= 8 and bs % 16 == 0
    assert TMC % bs == 0 and mp % ms == 0

    x = jnp.concatenate([x_prompt.reshape(mp, d), x_sample.transpose(1, 0, 2).reshape(ms, d)], axis=0)
    c_all = jnp.concatenate([c_prompt, jnp.zeros((8 - bp, d), F32), c_sample], axis=0)
    ada = _ada_all(c_all, w_ada, b_ada)
    n_pad = ROUTE_LANES - N_GROUPS - N_EXPERTS
    wr_all = jnp.concatenate([moe_w_group, moe_w_expert, jnp.zeros((depth, d, n_pad), F32)], axis=2)
    br_all = jnp.concatenate([moe_b_group, moe_b_expert, jnp.zeros((depth, n_pad), F32)], axis=1)

    sample_init = [(state_l0_ssm_re, state_l0_ssm_im), (cache_l1_conformer_conv,), (cache_l2_short_conv,),
                   (state_l3_ssm_re, state_l3_ssm_im)]
    p_states, s_states = [], []

    def mixer_in_dtype(layer):
        return F32 if layer % 3 == 0 else BF16

    h = _modulate(x, norm1_g[0:1], ada[0, 0:8], ada[0, 8:], 0, 1, mp, lp, mixer_in_dtype(0))
    for i in range(depth):
        ap, as_ = ada[i, 0:8], ada[i, 8:]
        kind, j = i % 3, i // 3
        if kind == 0:
            params = _s5_params(ssm_a_re[j], ssm_a_im[j], ssm_log_dt[j], ssm_b_re[j], ssm_b_im[j],
                                ssm_c_re[j], ssm_c_im[j])
            dsk = ssm_d[j:j + 1]
            pp, pre, pim = _s5_prompt(h, bp, lp, params, dsk)
            h0re, h0im = sample_init[i]
            psm, sre, sim = _s5_sample(h, mp, bs, ls, h0re, h0im, params, dsk)
            p_states.append((pre[:bp].reshape(bp, g_ssm, p_ssm), pim[:bp].reshape(bp, g_ssm, p_ssm)))
            s_states.append((sre, sim))
            pre_out = (pp.reshape(mp, d), psm)
            w_out = ssm_w_glu
        elif kind == 1:
            gl = _mm_split((h,), cf_w_pw1, j, 2, _mm_glu_kernel, 1, 512, mp, "mm_glu")
            extra = (cf_b_dw, cf_ln_g, cf_ln_b)
            pp = _conv_prompt(gl, bp, lp, cf_w_dw, j, "cf", extra)
            (cache,) = sample_init[i]
            g_tm = gl[mp:].reshape(ls, bs, d)
            psm = _conv_sample(g_tm, cache.transpose(1, 0, 2), cf_w_dw, j, "cf", extra)
            hist = CF_WIDTH - 1
            p_states.append((jnp.stack([gl[b * lp + lp - hist:(b + 1) * lp] for b in range(bp)]),))
            s_states.append((jnp.concatenate([cache, g_tm.transpose(1, 0, 2)], axis=1)[:, -hist:],))
            pre_out = (pp, psm.reshape(ms, d))
            w_out = cf_w_pw2
        else:
            bg, v = _mm_split((h,), sc_w_in, j, 3, _mm_bcv_kernel, 2, 512, mp, "mm_bcv")
            pp = _conv_prompt(v, bp, lp, sc_w_conv, j, "sc", (bg,))
            (cache,) = sample_init[i]
            v_tm = v[mp:].reshape(ls, bs, d)
            psm = _conv_sample(v_tm, cache.transpose(1, 0, 2), sc_w_conv, j, "sc", (bg[mp:].reshape(ls, bs, d),))
            hist = SC_WIDTH - 1
            p_states.append((jnp.stack([v[b * lp + lp - hist:(b + 1) * lp] for b in range(bp)]),))
            s_states.append((jnp.concatenate([cache, v_tm.transpose(1, 0, 2)], axis=1)[:, -hist:],))
            pre_out = (pp, psm.reshape(ms, d))
            w_out = sc_w_out

        x = _mm_resid(pre_out, w_out, j, x, ap, as_, 2, mp, lp, glu=(kind == 0))

        h2, route = _modulate(x, norm2_g[i:i + 1], ap, as_, 3, 4, mp, lp, F32, router=(wr_all[i], br_all[i:i + 1]))
        if i + 1 < depth:
            nxt = (norm1_g[i + 1:i + 2], ada[i + 1, 0:8], ada[i + 1, 8:], mixer_in_dtype(i + 1))
            x, h = _moe_layer(x, h2, route, moe_w13, moe_w2, i, ap, as_, mp, lp, nxt=nxt)
        else:
            y_p, y_s = _moe_layer(x, h2, route, moe_w13, moe_w2, i, ap, as_, mp, lp,
                                  final_g=final_norm_g.reshape(1, d))

    y_prompt = y_p.reshape(bp, lp, d)
    y_sample = y_s.reshape(ls, bs, d).transpose(1, 0, 2)
    ps, ss = p_states, s_states
    return (y_prompt, y_sample,
            ps[0][0], ps[0][1], ps[1][0], ps[2][0], ps[3][0], ps[3][1],
            ss[0][0], ss[0][1], ss[1][0], ss[2][0], ss[3][0], ss[3][1])
```

```python
import functools

import jax
import jax.numpy as jnp
from jax import lax
from jax.experimental import pallas as pl
from jax.experimental.pallas import tpu as pltpu

F32 = jnp.float32
BF16 = jnp.bfloat16
I32 = jnp.int32

D_MODEL = 2048
SSM_GROUP = 16
SSM_STATE = 64
SSM_GB = 16
SSM_CH = SSM_GB * SSM_GROUP
SSM_ST = SSM_GB * SSM_STATE
CF_WIDTH = 31
SC_WIDTH = 3
N_GROUPS = 8
EPG = 8
N_EXPERTS = N_GROUPS * EPG
D_EXPERT = D_MODEL // 4
EPS = 1e-6

TM = 512
MOE_BM = 128
TMC = 256
TMS = 512
PLAN_TB = 512
PLAN_ROWS = 256
ROUTE_LANES = 128
VMEM_LIMIT = 56 * 1024 * 1024


def _cparams(sem):
    return pltpu.CompilerParams(dimension_semantics=sem, vmem_limit_bytes=VMEM_LIMIT)


def _dot(a, b):
    return jnp.dot(a, b, preferred_element_type=F32)


def _dotb(a, b):
    return _dot(a.astype(BF16), b.astype(BF16))


def _sigmoid(x):
    return 1.0 / (1.0 + jnp.exp(-x))


def _silu(x):
    return x * _sigmoid(x)


def _gelu_tanh(x):
    return 0.5 * x * (1.0 + jnp.tanh(0.7978845608028654 * (x + 0.044715 * (x * x * x))))


def _seq_vec(vp_ref, vs_ref, blk, n_prompt_blocks, blocks_per_batch, rows):
    b = jnp.minimum(blk // blocks_per_batch, vp_ref.shape[0] - 1)
    vp = vp_ref[pl.ds(b, 1), :]
    vs = vs_ref[...]
    reps = rows // vs.shape[0]
    if reps > 1:
        vs = jnp.concatenate([vs] * reps, axis=0)
    return jnp.where(blk < n_prompt_blocks, vp, vs)


def _ada_kernel(c_ref, w_ref, b_ref, o_ref):
    c = c_ref[...]
    o_ref[0] = _dotb(_silu(c), w_ref[0]) + b_ref[0]


def _ada_all(c_all, w_ada, b_ada):
    depth, d, n = w_ada.shape
    rows = c_all.shape[0]
    tn = 1024
    return pl.pallas_call(
        _ada_kernel,
        out_shape=jax.ShapeDtypeStruct((depth, rows, n), F32),
        grid=(depth, n // tn),
        in_specs=[
            pl.BlockSpec((rows, d), lambda l, j: (0, 0)),
            pl.BlockSpec((1, d, tn), lambda l, j: (l, 0, j)),
            pl.BlockSpec((1, 1, tn), lambda l, j: (l, 0, j)),
        ],
        out_specs=pl.BlockSpec((1, rows, tn), lambda l, j: (l, 0, j)),
        compiler_params=_cparams(("parallel", "parallel")),
        name="ada_proj",
    )(c_all, w_ada, b_ada.reshape(depth, 1, n))


def _rms(x, g):
    return x * lax.rsqrt(jnp.mean(x * x, axis=-1, keepdims=True) + EPS) * g


def _modulate_kernel(x_ref, g_ref, shp_ref, scp_ref, shs_ref, scs_ref, o_ref, *, npb, bpb):
    i = pl.program_id(0)
    rows = x_ref.shape[0]
    y = _rms(x_ref[...], g_ref[...])
    sc = _seq_vec(scp_ref, scs_ref, i, npb, bpb, rows)
    sh = _seq_vec(shp_ref, shs_ref, i, npb, bpb, rows)
    o_ref[...] = (y * (1.0 + sc) + sh).astype(o_ref.dtype)


def _route_from_logits(lg):
    rows = lg.shape[0]
    lane = lax.broadcasted_iota(I32, (rows, ROUTE_LANES), 1)
    lane_f = lane.astype(F32)
    neg = jnp.float32(-jnp.inf)
    big = jnp.float32(ROUTE_LANES)
    is_g = lane < N_GROUPS
    gl = jnp.where(is_g, lg, neg)
    gmax = jnp.max(gl, axis=1, keepdims=True)
    grp = jnp.min(jnp.where(gl == gmax, lane_f, big), axis=1, keepdims=True)
    p_grp = 1.0 / jnp.sum(jnp.where(is_g, jnp.exp(gl - gmax), 0.0), axis=1, keepdims=True)
    e_grp = ((lane - N_GROUPS) >> 3).astype(F32)
    valid = (lane >= N_GROUPS) & (lane < N_GROUPS + N_EXPERTS) & (e_grp == grp)
    el = jnp.where(valid, lg, neg)
    v1 = jnp.max(el, axis=1, keepdims=True)
    i1 = jnp.min(jnp.where(el == v1, lane_f, big), axis=1, keepdims=True)
    el2 = jnp.where(lane_f == i1, neg, el)
    v2 = jnp.max(el2, axis=1, keepdims=True)
    i2 = jnp.min(jnp.where(el2 == v2, lane_f, big), axis=1, keepdims=True)
    t = jnp.exp(v2 - v1)
    den = 1.0 + t
    w1 = (1.0 / den) * p_grp
    w2 = (t / den) * p_grp
    out = jnp.where(lane == 0, i1 - N_GROUPS, 0.0)
    out = jnp.where(lane == 1, i2 - N_GROUPS, out)
    out = jnp.where(lane == 2, w1, out)
    out = jnp.where(lane == 3, w2, out)
    return out


def _modulate_route_kernel(x_ref, g_ref, shp_ref, scp_ref, shs_ref, scs_ref, wr_ref, br_ref,
                           o_ref, r_ref, *, npb, bpb):
    i = pl.program_id(0)
    rows = x_ref.shape[0]
    y = _rms(x_ref[...], g_ref[...])
    sc = _seq_vec(scp_ref, scs_ref, i, npb, bpb, rows)
    sh = _seq_vec(shp_ref, shs_ref, i, npb, bpb, rows)
    h = y * (1.0 + sc) + sh
    o_ref[...] = h
    lg = _dotb(h, wr_ref[...]) + br_ref[...]
    r_ref[...] = _route_from_logits(lg)


def _modulate(x, g, ap, as_, sh_idx, sc_idx, mp, lp, out_dtype, router=None):
    m, d = x.shape
    bs = as_.shape[0]
    npb, bpb = mp // TM, lp // TM
    in_specs = [
        pl.BlockSpec((TM, d), lambda i: (i, 0)),
        pl.BlockSpec((1, d), lambda i: (0, 0)),
        pl.BlockSpec((8, d), lambda i: (0, sh_idx)),
        pl.BlockSpec((8, d), lambda i: (0, sc_idx)),
        pl.BlockSpec((bs, d), lambda i: (0, sh_idx)),
        pl.BlockSpec((bs, d), lambda i: (0, sc_idx)),
    ]
    if router is None:
        return pl.pallas_call(
            functools.partial(_modulate_kernel, npb=npb, bpb=bpb),
            out_shape=jax.ShapeDtypeStruct((m, d), out_dtype),
            grid=(m // TM,),
            in_specs=in_specs,
            out_specs=pl.BlockSpec((TM, d), lambda i: (i, 0)),
            compiler_params=_cparams(("parallel",)),
            name="modulate",
        )(x, g, ap, ap, as_, as_)
    wr, br = router
    return pl.pallas_call(
        functools.partial(_modulate_route_kernel, npb=npb, bpb=bpb),
        out_shape=(jax.ShapeDtypeStruct((m, d), F32), jax.ShapeDtypeStruct((m, ROUTE_LANES), F32)),
        grid=(m // TM,),
        in_specs=in_specs + [
            pl.BlockSpec((d, ROUTE_LANES), lambda i: (0, 0)),
            pl.BlockSpec((1, ROUTE_LANES), lambda i: (0, 0)),
        ],
        out_specs=(pl.BlockSpec((TM, d), lambda i: (i, 0)), pl.BlockSpec((TM, ROUTE_LANES), lambda i: (i, 0))),
        compiler_params=_cparams(("parallel",)),
        name="modulate_route",
    )(x, g, ap, ap, as_, as_, wr, br)


def _a_specs(a_parts, npb):
    k = a_parts[0].shape[1]
    if len(a_parts) == 1:
        return [pl.BlockSpec((TM, k), lambda j, i: (i, 0))]
    return [pl.BlockSpec((TM, k), lambda j, i: (jnp.minimum(i, npb - 1), 0)),
            pl.BlockSpec((TM, k), lambda j, i: (jnp.maximum(i - npb, 0), 0))]


def _mm_products(a_refs, w_refs, wbf_ref, npb):
    i = pl.program_id(1)

    @pl.when(i == 0)
    def _():
        for s, w_ref in enumerate(w_refs):
            wbf_ref[s] = w_ref[...].astype(BF16)

    if len(a_refs) == 1:
        a = a_refs[0][...]
    else:
        a = jnp.where(i < npb, a_refs[0][...], a_refs[1][...])
    a = a.astype(BF16)
    return [_dot(a, wbf_ref[s]) for s in range(len(w_refs))]


def _mm_glu_kernel(*refs, n_a, npb):
    a_refs, (wa_ref, wb_ref, o_ref, wbf_ref) = refs[:n_a], refs[n_a:]
    za, zb = _mm_products(a_refs, (wa_ref, wb_ref), wbf_ref, npb)
    o_ref[...] = za * _sigmoid(zb)


def _mm_bcv_kernel(*refs, n_a, npb):
    a_refs, (w0_ref, w1_ref, w2_ref, bg_ref, v_ref, wbf_ref) = refs[:n_a], refs[n_a:]
    bg, cg, hv = _mm_products(a_refs, (w0_ref, w1_ref, w2_ref), wbf_ref, npb)
    bg_ref[...] = bg
    v_ref[...] = cg * hv


def _mm_resid_kernel(*refs, n_a, npb, bpb, glu):
    a_refs, rest = refs[:n_a], refs[n_a:]
    n_w = 2 if glu else 1
    w_refs = rest[:n_w]
    x_ref, gp_ref, gs_ref, o_ref, wbf_ref = rest[n_w:]
    z = _mm_products(a_refs, w_refs, wbf_ref, npb)
    z = z[0] * _sigmoid(z[1]) if glu else z[0]
    gate = _seq_vec(gp_ref, gs_ref, pl.program_id(1), npb, bpb, x_ref.shape[0])
    o_ref[...] = x_ref[...] + gate * z


def _w_spec(k, tn, layer, col0):
    return pl.BlockSpec((None, k, tn), lambda j, i: (layer, 0, col0 + j))


def _mm_split(a_parts, w, layer, n_split, kernel_fn, n_out, tn, mp, name):
    m = sum(a.shape[0] for a in a_parts)
    k = a_parts[0].shape[1]
    n = w.shape[2] // n_split
    nj = n // tn
    out_shape = [jax.ShapeDtypeStruct((m, n), F32) for _ in range(n_out)]
    out_specs = [pl.BlockSpec((TM, tn), lambda j, i: (i, j)) for _ in range(n_out)]
    return pl.pallas_call(
        functools.partial(kernel_fn, n_a=len(a_parts), npb=mp // TM),
        out_shape=out_shape if n_out > 1 else out_shape[0],
        grid=(nj, m // TM),
        in_specs=_a_specs(a_parts, mp // TM) + [_w_spec(k, tn, layer, s * nj) for s in range(n_split)],
        out_specs=out_specs if n_out > 1 else out_specs[0],
        scratch_shapes=[pltpu.VMEM((n_split, k, tn), BF16)],
        compiler_params=_cparams(("parallel", "arbitrary")),
        name=name,
    )(*a_parts, *([w] * n_split))


def _mm_resid(a_parts, w, layer, x, ap, as_, gate_idx, mp, lp, glu=False):
    m = x.shape[0]
    k = a_parts[0].shape[1]
    n_split = 2 if glu else 1
    n = w.shape[2] // n_split
    tn = 512 if glu else 1024
    nj = n // tn
    bs = as_.shape[0]
    return pl.pallas_call(
        functools.partial(_mm_resid_kernel, n_a=len(a_parts), npb=mp // TM, bpb=lp // TM, glu=glu),
        out_shape=jax.ShapeDtypeStruct((m, n), F32),
        grid=(nj, m // TM),
        in_specs=_a_specs(a_parts, mp // TM)
        + [_w_spec(k, tn, layer, s * nj) for s in range(n_split)]
        + [
            pl.BlockSpec((TM, tn), lambda j, i: (i, j)),
            pl.BlockSpec((8, tn), lambda j, i: (0, gate_idx * nj + j)),
            pl.BlockSpec((bs, tn), lambda j, i: (0, gate_idx * nj + j)),
        ],
        out_specs=pl.BlockSpec((TM, tn), lambda j, i: (i, j)),
        scratch_shapes=[pltpu.VMEM((n_split, k, tn), BF16)],
        compiler_params=_cparams(("parallel", "arbitrary")),
        name="mm_glu_resid" if glu else "mm_resid",
    )(*a_parts, *([w] * n_split), x, ap, as_)


S5_LC = 512


def _s5_scan(bu_ref, st_ref, are_ref, aim_ref, nb, tc):
    n_tiles = nb // 8

    def tile(r, carry):
        r0 = pl.multiple_of(r * 8, 8)
        for c in range(SSM_ST // S5_LC):
            cre = pl.ds(c * S5_LC, S5_LC)
            cim = pl.ds(SSM_ST + c * S5_LC, S5_LC)
            ar = are_ref[:, cre]
            ai = aim_ref[:, cre]
            srow = pl.ds(r0, 8)
            hr = st_ref[srow, cre]
            hi = st_ref[srow, cim]

            def step(t, carry):
                hr, hi = carry
                row = pl.ds(pl.multiple_of(t * nb + r0, 8), 8)
                nr = ar * hr - ai * hi + bu_ref[row, cre]
                ni = ar * hi + ai * hr + bu_ref[row, cim]
                bu_ref[row, cre] = nr
                bu_ref[row, cim] = ni
                return nr, ni

            hr, hi = lax.fori_loop(0, tc, step, (hr, hi), unroll=8)
            st_ref[srow, cre] = hr
            st_ref[srow, cim] = hi
        return carry

    lax.fori_loop(0, n_tiles, tile, 0)


def _s5_core(u, bu_ref, st_ref, h0re_ref, h0im_ref, are_ref, aim_ref, bw_ref, cw_ref, dsk_ref,
             sre_ref, sim_ref, scan):
    t = pl.program_id(1)

    @pl.when(t == 0)
    def _():
        for gl in range(SSM_GB):
            st_ref[:, gl * SSM_STATE:(gl + 1) * SSM_STATE] = h0re_ref[:, gl, :]
            st_ref[:, SSM_ST + gl * SSM_STATE:SSM_ST + (gl + 1) * SSM_STATE] = h0im_ref[:, gl, :]

    bu_ref[...] = _dotb(u, bw_ref[...])
    scan(bu_ref, st_ref, are_ref, aim_ref)
    y = _dot(bu_ref[...].astype(BF16), cw_ref[...].astype(BF16)) + dsk_ref[...] * u

    @pl.when(t == pl.num_programs(1) - 1)
    def _():
        for gl in range(SSM_GB):
            sre_ref[:, gl, :] = st_ref[:, gl * SSM_STATE:(gl + 1) * SSM_STATE]
            sim_ref[:, gl, :] = st_ref[:, SSM_ST + gl * SSM_STATE:SSM_ST + (gl + 1) * SSM_STATE]

    return _gelu_tanh(y)


S5_NSUB = 4


def _s5_scan_pairs_sub(bu_ref, h_ref, ar, ai, carry):
    n_lc = len(ar)
    first_half = lax.broadcasted_iota(I32, (8, S5_LC), 0) < 4
    for j in range(bu_ref.shape[0] // 8):
        rows = slice(j * 8, j * 8 + 8)
        new = []
        for c in range(n_lc):
            cre = slice(c * S5_LC, (c + 1) * S5_LC)
            cim = slice(SSM_ST + c * S5_LC, SSM_ST + (c + 1) * S5_LC)
            hr, hi = carry[c]
            br = bu_ref[rows, cre]
            bi = bu_ref[rows, cim]
            er = ar[c] * hr - ai[c] * hi + br
            ei = ar[c] * hi + ai[c] * hr + bi
            sr = pltpu.roll(er, 4, 0)
            si = pltpu.roll(ei, 4, 0)
            orr = ar[c] * sr - ai[c] * si + br
            oi = ar[c] * si + ai[c] * sr + bi
            h_ref[rows, cre] = jnp.where(first_half, er, orr)
            h_ref[rows, cim] = jnp.where(first_half, ei, oi)
            new.append((pltpu.roll(orr, 4, 0), pltpu.roll(oi, 4, 0)))
        carry = new
    return carry


def _s5_prompt_kernel(*refs, nbatch, tc):
    u_refs = refs[:nbatch]
    (h0re_ref, h0im_ref, are_ref, aim_ref, bw_ref, cw_ref, dsk_ref,
     p_ref, sre_ref, sim_ref, r_ref, st_ref, y_ref) = refs[nbatch:nbatch + 13]
    bu_refs = refs[nbatch + 13:nbatch + 13 + S5_NSUB]
    h_refs = refs[nbatch + 13 + S5_NSUB:]
    t = pl.program_id(1)

    @pl.when(t == 0)
    def _():
        st_ref[:, 0:SSM_ST] = h0re_ref[...]
        st_ref[:, SSM_ST:2 * SSM_ST] = h0im_ref[...]

    nslab = r_ref.shape[0]
    for b in range(nbatch):
        ub = u_refs[b][...]
        for s in range(nslab):
            r_ref[s, pl.ds(b, tc, stride=nbatch), :] = ub[:, s * 128:(s + 1) * 128]
    u = jnp.concatenate([r_ref[s] for s in range(nslab)], axis=1)
    u_bf = u.astype(BF16)
    bw = bw_ref[...].astype(BF16)
    cw = cw_ref[...].astype(BF16)
    sub = u.shape[0] // S5_NSUB
    n_lc = SSM_ST // S5_LC
    ar = [are_ref[:, c * S5_LC:(c + 1) * S5_LC] for c in range(n_lc)]
    ai = [aim_ref[:, c * S5_LC:(c + 1) * S5_LC] for c in range(n_lc)]
    carry = [(st_ref[:, c * S5_LC:(c + 1) * S5_LC], st_ref[:, SSM_ST + c * S5_LC:SSM_ST + (c + 1) * S5_LC])
             for c in range(n_lc)]

    def b_proj(s):
        bu_refs[s][...] = _dot(u_bf[s * sub:(s + 1) * sub], bw)

    def c_proj(s):
        return _dot(h_refs[s][...].astype(BF16), cw)

    ys = []
    b_proj(0)
    for s in range(S5_NSUB):
        if s + 1 < S5_NSUB:
            b_proj(s + 1)
        carry = _s5_scan_pairs_sub(bu_refs[s], h_refs[s], ar, ai, carry)
        if s >= 1:
            ys.append(c_proj(s - 1))
    ys.append(c_proj(S5_NSUB - 1))
    for c in range(n_lc):
        st_ref[:, c * S5_LC:(c + 1) * S5_LC] = carry[c][0]
        st_ref[:, SSM_ST + c * S5_LC:SSM_ST + (c + 1) * S5_LC] = carry[c][1]
    y = _gelu_tanh(jnp.concatenate(ys, axis=0) + dsk_ref[...] * u)
    for s in range(nslab):
        y_ref[s] = y[:, s * 128:(s + 1) * 128]
    for b in range(nbatch):
        p_ref[b] = jnp.concatenate([y_ref[s, pl.ds(b, tc, stride=nbatch), :] for s in range(nslab)],
                                   axis=1).astype(p_ref.dtype)

    @pl.when(t == pl.num_programs(1) - 1)
    def _():
        sre_ref[...] = st_ref[:, 0:SSM_ST]
        sim_ref[...] = st_ref[:, SSM_ST:2 * SSM_ST]


def _s5_sample_kernel(u_ref, h0re_ref, h0im_ref, are_ref, aim_ref, bw_ref, cw_ref, dsk_ref,
                      p_ref, sre_ref, sim_ref, bu_ref, st_ref, *, nb, tc):
    scan = functools.partial(_s5_scan, nb=nb, tc=tc)
    p_ref[...] = _s5_core(u_ref[...], bu_ref, st_ref, h0re_ref, h0im_ref, are_ref, aim_ref, bw_ref, cw_ref,
                          dsk_ref, sre_ref, sim_ref, scan).astype(p_ref.dtype)


def _s5_params(a_re, a_im, log_dt, b_re, b_im, c_re, c_im):
    g, p = a_re.shape
    dt = jnp.exp(log_dt)[:, None]
    mag = jnp.exp(dt * a_re)
    ang = dt * a_im
    ab_re = mag * jnp.cos(ang)
    ab_im = mag * jnp.sin(ang)
    n_re = ab_re - 1.0
    n_im = ab_im
    den = a_re * a_re + a_im * a_im
    f_re = ((n_re * a_re + n_im * a_im) / den)[..., None]
    f_im = ((n_im * a_re - n_re * a_im) / den)[..., None]
    bb_re = f_re * b_re - f_im * b_im
    bb_im = f_re * b_im + f_im * b_re
    ngb = g // SSM_GB
    eye = jnp.eye(SSM_GB, dtype=F32)

    def bdiag_b(bb):
        t = bb.reshape(ngb, SSM_GB, p, SSM_GROUP).transpose(0, 1, 3, 2)
        t = t[:, :, :, None, :] * eye[None, :, None, :, None]
        return t.reshape(ngb, SSM_CH, SSM_ST).astype(BF16)

    def bdiag_c(cc):
        t = cc.reshape(ngb, SSM_GB, SSM_GROUP, p).transpose(0, 1, 3, 2)
        t = t[:, :, :, None, :] * eye[None, :, None, :, None]
        return t.reshape(ngb, SSM_ST, SSM_CH).astype(BF16)

    bw =jnp.concatenate([bdiag_b(bb_re), bdiag_b(bb_im)], axis=2)
    cw = jnp.concatenate([bdiag_c(c_re), -bdiag_c(c_im)], axis=1)
    are = jnp.broadcast_to(ab_re.reshape(1, g * p), (8, g * p))
    aim = jnp.broadcast_to(ab_im.reshape(1, g * p), (8, g * p))
    return are, aim, bw, cw


def _s5_common_specs(nb):
    return [
        pl.BlockSpec((nb, SSM_ST), lambda gb, t: (0, gb)),
        pl.BlockSpec((nb, SSM_ST), lambda gb, t: (0, gb)),
        pl.BlockSpec((8, SSM_ST), lambda gb, t: (0, gb)),
        pl.BlockSpec((8, SSM_ST), lambda gb, t: (0, gb)),
        pl.BlockSpec((None, SSM_CH, 2 * SSM_ST), lambda gb, t: (gb, 0, 0)),
        pl.BlockSpec((None, 2 * SSM_ST, SSM_CH), lambda gb, t: (gb, 0, 0)),
        pl.BlockSpec((1, SSM_CH), lambda gb, t: (0, gb)),
    ]


def _s5_prompt(h, nbatch, lp, params, dsk):
    assert nbatch == 4, "the paired scan packs two steps of 4 sequences into one 8-row tile"
    are, aim, bw, cw = params
    d = h.shape[1]
    ngb = d // SSM_CH
    tc = 256
    nt = lp // tc
    gp = are.shape[1]
    zeros = jnp.zeros((8, gp), F32)
    u_specs = [pl.BlockSpec((tc, SSM_CH), functools.partial(lambda gb, t, b: (b * nt + t, gb), b=b))
               for b in range(nbatch)]
    rows = tc * nbatch
    return pl.pallas_call(
        functools.partial(_s5_prompt_kernel, nbatch=nbatch, tc=tc),
        out_shape=(jax.ShapeDtypeStruct((nbatch, lp, d), BF16),
                   jax.ShapeDtypeStruct((8, gp), F32), jax.ShapeDtypeStruct((8, gp), F32)),
        grid=(ngb, nt),
        in_specs=u_specs + _s5_common_specs(8),
        out_specs=(pl.BlockSpec((nbatch, tc, SSM_CH), lambda gb, t: (0, t, gb)),
                   pl.BlockSpec((8, SSM_ST), lambda gb, t: (0, gb)),
                   pl.BlockSpec((8, SSM_ST), lambda gb, t: (0, gb))),
        scratch_shapes=[pltpu.VMEM((SSM_CH // 128, rows, 128), F32), pltpu.VMEM((8, 2 * SSM_ST), F32),
                        pltpu.VMEM((SSM_CH // 128, rows, 128), F32)]
        + [pltpu.VMEM((rows // S5_NSUB, 2 * SSM_ST), F32)] * (2 * S5_NSUB),
        compiler_params=_cparams(("parallel", "arbitrary")),
        name="s5_prompt",
    )(*([h] * nbatch), zeros, zeros, are, aim, bw, cw, dsk)


def _s5_sample(h, mp, bs, ls, h0re, h0im, params, dsk):
    are, aim, bw, cw = params
    d = h.shape[1]
    ngb = d // SSM_CH
    rows = ls * bs
    blk0 = mp // rows
    g, p = h0re.shape[1:]
    state_spec = pl.BlockSpec((bs, SSM_GB, p), lambda gb, t: (0, gb, 0))
    return pl.pallas_call(
        functools.partial(_s5_sample_kernel, nb=bs, tc=ls),
        out_shape=(jax.ShapeDtypeStruct((rows, d), BF16),
                   jax.ShapeDtypeStruct((bs, g, p), F32), jax.ShapeDtypeStruct((bs, g, p), F32)),
        grid=(ngb, 1),
        in_specs=[pl.BlockSpec((rows, SSM_CH), lambda gb, t: (blk0, gb)), state_spec, state_spec]
        + _s5_common_specs(bs)[2:],
        out_specs=(pl.BlockSpec((rows, SSM_CH), lambda gb, t: (0, gb)), state_spec, state_spec),
        scratch_shapes=[pltpu.VMEM((rows, 2 * SSM_ST), F32), pltpu.VMEM((bs, 2 * SSM_ST), F32)],
        compiler_params=_cparams(("parallel", "arbitrary")),
        name="s5_sample",
    )(h, h0re, h0im, are, aim, bw, cw, dsk)


def _ln_silu(y, g, b):
    mu = jnp.mean(y, axis=-1, keepdims=True)
    yc = y - mu
    z = yc * lax.rsqrt(jnp.mean(yc * yc, axis=-1, keepdims=True) + EPS) * g + b
    return _silu(z)


def _conv_prompt_kernel(cur_ref, halo_ref, buf_ref, w_ref, *rest, width, mode):
    if mode == "cf":
        bdw_ref, lng_ref, lnb_ref, o_ref, s_ref, acc_ref = rest
    else:
        bg_ref, o_ref, s_ref, acc_ref = rest
    t = pl.program_id(1)
    hb = halo_ref.shape[0]
    tc, d = cur_ref.shape
    first = t == 0
    s_ref[0:hb, :] = jnp.where(first, buf_ref[0], halo_ref[...])
    s_ref[hb:hb + tc, :] = cur_ref[...]
    s_ref[hb + tc:hb + tc + 8, :] = jnp.zeros((8, d), F32)
    off = hb - (width - 1)
    by_shift = [[(o // 8, o - off) for o in range(off, off + width) if o % 8 == sh] for sh in range(8)]
    rc, lch = 64, 256
    for r0 in range(0, tc, rc):
        for c in range(d // lch):
            cols = pl.ds(c * lch, lch)
            out = None
            for sh, taps in enumerate(by_shift):
                z = None
                for q, k in taps:
                    t = w_ref[k:k + 1, cols] * s_ref[r0 + 8 * q:r0 + 8 * q + rc + 8, cols]
                    z = t if z is None else z + t
                if z is not None:
                    zs = z[sh:sh + rc]
                    out = zs if out is None else out + zs
            acc_ref[r0:r0 + rc, cols] = out
    if mode == "cf":
        o_ref[...] = _ln_silu(acc_ref[...] + bdw_ref[...], lng_ref[...], lnb_ref[...]).astype(o_ref.dtype)
    else:
        o_ref[...] = (bg_ref[...] * acc_ref[...]).astype(o_ref.dtype)


def _conv_prompt(v, nbatch, lp, w, layer, mode, extra):
    d = v.shape[1]
    width = w.shape[1]
    hb = 32 if width > 9 else 8
    tc = 256
    nt = lp // tc
    zeros = jnp.zeros((nbatch, hb, d), F32)
    in_specs = [
        pl.BlockSpec((tc, d), lambda b, t: (b * nt + t, 0)),
        pl.BlockSpec((hb, d), lambda b, t: (jnp.maximum((b * lp + t * tc) // hb - 1, 0), 0)),
        pl.BlockSpec((1, hb, d), lambda b, t: (b, 0, 0)),
        pl.BlockSpec((None, width, d), lambda b, t: (layer, 0, 0)),
    ]
    if mode == "cf":
        in_specs += [pl.BlockSpec((1, d), lambda b, t: (layer, 0))] * 3
    else:
        in_specs += [pl.BlockSpec((tc, d), lambda b, t: (b * nt + t, 0))]
    return pl.pallas_call(
        functools.partial(_conv_prompt_kernel, width=width, mode=mode),
        out_shape=jax.ShapeDtypeStruct((nbatch * lp, d), BF16),
        grid=(nbatch, nt),
        in_specs=in_specs,
        out_specs=pl.BlockSpec((tc, d), lambda b, t: (b * nt + t, 0)),
        scratch_shapes=[pltpu.VMEM((hb + tc + 8, d), F32), pltpu.VMEM((tc, d), F32)],
        compiler_params=_cparams(("parallel", "arbitrary")),
        name="conv_prompt_" + mode,
    )(v, v, zeros, w, *extra)


def _conv_sample_kernel(v_ref, cache_ref, w_ref, *rest, width, mode):
    if mode == "cf":
        bdw_ref, lng_ref, lnb_ref, o_ref = rest
    else:
        bg_ref, o_ref = rest
    ls = v_ref.shape[0]
    hist = width - 1

    def full(j):
        return cache_ref[j] if j < hist else v_ref[j - hist]

    for l in range(ls):
        acc = w_ref[0:1, :] * full(l)
        for k in range(1, width):
            acc = acc + w_ref[k:k + 1, :] * full(l + k)
        if mode == "cf":
            o_ref[l] = _ln_silu(acc + bdw_ref[...], lng_ref[...], lnb_ref[...]).astype(o_ref.dtype)
        else:
            o_ref[l] = (bg_ref[l] * acc).astype(o_ref.dtype)


def _conv_sample(v_tm, cache_tm, w, layer, mode, extra):
    ls, bs, d = v_tm.shape
    width = w.shape[1]
    bc = 16
    in_specs = [
        pl.BlockSpec((ls, bc, d), lambda i: (0, i, 0)),
        pl.BlockSpec((width - 1, bc, d), lambda i: (0, i, 0)),
        pl.BlockSpec((None, width, d), lambda i: (layer, 0, 0)),
    ]
    if mode == "cf":
        in_specs += [pl.BlockSpec((1, d), lambda i: (layer, 0))] * 3
    else:
        in_specs += [pl.BlockSpec((ls, bc, d), lambda i: (0, i, 0))]
    return pl.pallas_call(
        functools.partial(_conv_sample_kernel, width=width, mode=mode),
        out_shape=jax.ShapeDtypeStruct((ls, bs, d), BF16),
        grid=(bs // bc,),
        in_specs=in_specs,
        out_specs=pl.BlockSpec((ls, bc, d), lambda i: (0, i, 0)),
        compiler_params=_cparams(("parallel",)),
        name="conv_sample_" + mode,
    )(v_tm, cache_tm, w, *extra)


def _onehot2(r):
    lane = lax.broadcasted_iota(I32, r.shape, 1).astype(F32)
    return jnp.where((lane == r[:, 0:1]) | (lane == r[:, 1:2]), 1.0, 0.0)


def _moe_plan_kernel(r_ref, pos_ref, info_ref, cnt_ref, run_ref, pst_ref):
    ph = pl.program_id(0)
    i = pl.program_id(1)
    r = r_ref[...]
    tb = r.shape[0]
    oh = _onehot2(r)
    colsum = jnp.broadcast_to(jnp.sum(oh, axis=0, keepdims=True), (8, ROUTE_LANES))

    @pl.when((ph == 0) & (i == 0))
    def _():
        cnt_ref[...] = jnp.zeros_like(cnt_ref)

    @pl.when(ph == 0)
    def _():
        cnt_ref[...] += colsum

    @pl.when((ph == 1) & (i == 0))
    def _():
        cnt = cnt_ref[...]
        nblk = jnp.floor((cnt + (MOE_BM - 1)) * (1.0 / MOE_BM))
        li = lax.broadcasted_iota(I32, (ROUTE_LANES, ROUTE_LANES), 0)
        lj = lax.broadcasted_iota(I32, (ROUTE_LANES, ROUTE_LANES), 1)
        upper = jnp.where(li <= lj, 1.0, 0.0).astype(BF16)
        bend = _dot(nblk.astype(BF16), upper)
        pst_ref[...] = (bend - nblk) * MOE_BM
        run_ref[...] = jnp.zeros_like(run_ref)
        n_used = bend[0:1, N_EXPERTS - 1:N_EXPERTS]
        bi = lax.broadcasted_iota(I32, (PLAN_ROWS, ROUTE_LANES), 0).astype(F32)
        ln = lax.broadcasted_iota(I32, (PLAN_ROWS, ROUTE_LANES), 1)
        ln_f = ln.astype(F32)
        pe = jnp.broadcast_to(bend[0:1, :], (PLAN_ROWS, ROUTE_LANES))
        pk = jnp.broadcast_to(nblk[0:1, :], (PLAN_ROWS, ROUTE_LANES))
        bcl = jnp.minimum(bi, n_used - 1.0)
        blk_e = jnp.sum(jnp.where((ln < N_EXPERTS) & (pe <= bcl), 1.0, 0.0), axis=1, keepdims=True)
        bend_col = jnp.sum(jnp.where(ln_f == bi, pe, 0.0), axis=1, keepdims=True)
        nblk_col = jnp.sum(jnp.where(ln_f == bi, pk, 0.0), axis=1, keepdims=True)
        info = jnp.where(ln == 0, blk_e, 0.0)
        info = jnp.where(ln == 1, n_used, info)
        info = jnp.where(ln == 2, bend_col, info)
        info = jnp.where(ln == 3, nblk_col, info)
        nonempty = (ln < N_EXPERTS) & (pk > 0.0)
        seq = jnp.sum(jnp.where(nonempty & (ln_f < blk_e), 1.0, 0.0), axis=1, keepdims=True)
        nxt = jnp.min(jnp.where(nonempty & (ln_f > blk_e), ln_f, float(ROUTE_LANES)), axis=1, keepdims=True)
        info = jnp.where(ln == 4, seq, info)
        info = jnp.where(ln == 5, nxt, info)
        info_ref[...] = info.astype(I32)

    @pl.when(ph == 1)
    def _():
        ti = lax.broadcasted_iota(I32, (tb, tb), 0)
        tj = lax.broadcasted_iota(I32, (tb, tb), 1)
        lower = jnp.where(ti > tj, 1.0, 0.0).astype(BF16)
        before = _dot(lower, oh.astype(BF16))
        val = pst_ref[0:1, :] + run_ref[0:1, :] + before
        lane = lax.broadcasted_iota(I32, r.shape, 1)
        lane_f = lane.astype(F32)
        p0 = jnp.sum(jnp.where(lane_f == r[:, 0:1], val, 0.0), axis=1, keepdims=True)
        p1 = jnp.sum(jnp.where(lane_f == r[:, 1:2], val, 0.0), axis=1, keepdims=True)
        pos_ref[...] = jnp.where(lane == 0, p0, jnp.where(lane == 1, p1, 0.0)).astype(I32)
        run_ref[...] += colsum


def _moe_plan(route):
    m = route.shape[0]
    nb = m // PLAN_TB
    return pl.pallas_call(
        _moe_plan_kernel,
        out_shape=(jax.ShapeDtypeStruct((m, ROUTE_LANES), I32), jax.ShapeDtypeStruct((PLAN_ROWS, ROUTE_LANES), I32)),
        grid=(2, nb),
        in_specs=[pl.BlockSpec((PLAN_TB, ROUTE_LANES), lambda ph, i: (i, 0))],
        out_specs=(pl.BlockSpec((PLAN_TB, ROUTE_LANES), lambda ph, i: (ph * i, 0)),
                   pl.BlockSpec((PLAN_ROWS, ROUTE_LANES), lambda ph, i: (0, 0))),
        scratch_shapes=[pltpu.VMEM((8, ROUTE_LANES), F32)] * 3,
        compiler_params=_cparams(("arbitrary", "arbitrary")),
        name="moe_plan",
    )(route)


def _moe_scatter_kernel(pos_ref, bend_ref, nblk_ref, nu_ref, h_ref, xs_ref, z_ref, sem, zsem):
    i = pl.program_id(0)
    n_blocks = xs_ref.shape[0] // MOE_BM
    rows = h_ref.shape[0]

    @pl.when(i == 0)
    def _():
        z_ref[...] = jnp.zeros_like(z_ref)

        def zero_copy(blk):
            start = pl.multiple_of(blk * MOE_BM, MOE_BM)
            return pltpu.make_async_copy(z_ref, xs_ref.at[pl.ds(start, MOE_BM)], zsem)

        def for_each_pad_block(fn):
            def per_expert(e, c):
                @pl.when(nblk_ref[e] > 0)
                def _():
                    fn(zero_copy(bend_ref[e] - 1))
                return c

            def per_tail_block(blk, c):
                @pl.when(blk >= nu_ref[0])
                def _():
                    fn(zero_copy(blk))
                return c

            lax.fori_loop(0, N_EXPERTS, per_expert, 0)
            lax.fori_loop(0, n_blocks, per_tail_block, 0)

        for_each_pad_block(lambda c: c.start())
        for_each_pad_block(lambda c: c.wait())

    def row(r, c):
        base = (i * rows + r) * 2
        for k in range(2):
            p = pos_ref[base + k]
            pltpu.make_async_copy(h_ref.at[pl.ds(r, 1)], xs_ref.at[pl.ds(p, 1)], sem).start()
        return c

    lax.fori_loop(0, rows, row, 0, unroll=4)
    for k in range(2):
        pltpu.make_async_copy(h_ref, xs_ref.at[pl.ds(0, rows)], sem).wait()


def _moe_scatter(h, pos2, bend, nblk, n_used, n_rows):
    m, d = h.shape
    grid_spec = pltpu.PrefetchScalarGridSpec(
        num_scalar_prefetch=4,
        grid=(m // TMS,),
        in_specs=[pl.BlockSpec((TMS, d), lambda i, *_: (i, 0))],
        out_specs=pl.BlockSpec(memory_space=pl.ANY),
        scratch_shapes=[pltpu.VMEM((MOE_BM, d), F32), pltpu.SemaphoreType.DMA(()), pltpu.SemaphoreType.DMA(())],
    )
    return pl.pallas_call(
        _moe_scatter_kernel,
        out_shape=jax.ShapeDtypeStruct((n_rows, d), F32),
        grid_spec=grid_spec,
        compiler_params=_cparams(("arbitrary",)),
        name="moe_scatter",
    )(pos2, bend, nblk, n_used, h)


def _moe_expert_kernel(be_ref, nu_ref, seq_ref, nxt_ref, xs_ref, w13_hbm, w2_hbm, y_ref,
                       w13f_ref, w2f_ref, w13b_ref, w2b_ref, sem13, sem2, *, layer):
    i = pl.program_id(0)
    used = i < nu_ref[0]
    e = be_ref[i]
    new_expert = (i == 0) | (e != be_ref[jnp.maximum(i - 1, 0)])
    slot = seq_ref[i] % 2

    def fetch(expert, s):
        return (pltpu.make_async_copy(w13_hbm.at[layer, expert], w13f_ref.at[s], sem13.at[s]),
                pltpu.make_async_copy(w2_hbm.at[layer, expert], w2f_ref.at[s], sem2.at[s]))

    @pl.when(i == 0)
    def _():
        for c in fetch(e, slot):
            c.start()

    @pl.when(used & new_expert)
    def _():
        nxt = nxt_ref[i]

        @pl.when(nxt < N_EXPERTS)
        def _():
            for c in fetch(nxt, 1 - slot):
                c.start(priority=1)

        for c in fetch(e, slot):
            c.wait()
        w13b_ref[...] = w13f_ref[slot].astype(BF16)
        w2b_ref[...] = w2f_ref[slot].astype(BF16)

    @pl.when(used)
    def _():
        ab = _dot(xs_ref[...].astype(BF16), w13b_ref[...])
        a = ab[:, 0:D_EXPERT]
        b = ab[:, D_EXPERT:2 * D_EXPERT]
        y_ref[...] = _dot((_silu(a) * b).astype(BF16), w2b_ref[...])

    @pl.when(jnp.logical_not(used))
    def _():
        y_ref[...] = jnp.zeros_like(y_ref)


def _moe_experts(xs, blk_e, n_used, seq, nxt, w13, w2, layer):
    n_rows, d = xs.shape
    n_blocks = blk_e.shape[0]
    de2 = w13.shape[3]
    grid_spec = pltpu.PrefetchScalarGridSpec(
        num_scalar_prefetch=4,
        grid=(n_blocks,),
        in_specs=[
            pl.BlockSpec((MOE_BM, d), lambda i, be, nu, sq, nx: (jnp.minimum(i, nu[0] - 1), 0)),
            pl.BlockSpec(memory_space=pl.ANY),
            pl.BlockSpec(memory_space=pl.ANY),
        ],
        out_specs=pl.BlockSpec((MOE_BM, d), lambda i, be, nu, sq, nx: (i, 0)),
        scratch_shapes=[pltpu.VMEM((2, d, de2), F32), pltpu.VMEM((2, de2 // 2, d), F32),
                        pltpu.VMEM((d, de2), BF16), pltpu.VMEM((de2 // 2, d), BF16),
                        pltpu.SemaphoreType.DMA((2,)), pltpu.SemaphoreType.DMA((2,))],
    )
    return pl.pallas_call(
        functools.partial(_moe_expert_kernel, layer=layer),
        out_shape=jax.ShapeDtypeStruct((n_rows, d), F32),
        grid_spec=grid_spec,
        compiler_params=_cparams(("arbitrary",)),
        name="moe_experts",
    )(blk_e, n_used, seq, nxt, xs, w13, w2)


def _moe_combine_rows(pos_ref, y_ref, x_ref, r_ref, gp_ref, gs_ref, ybuf, sem, npb, bpb):
    i = pl.program_id(0)
    n = pl.num_programs(0)
    slot = i % 2

    def gather(blk, s):
        def row(r, c):
            base = (blk * TMC + r) * 2
            for k in range(2):
                p = pos_ref[base + k]
                pltpu.make_async_copy(y_ref.at[pl.ds(p, 1)], ybuf.at[s, k, pl.ds(r, 1)], sem.at[s]).start()
            return c
        lax.fori_loop(0, TMC, row, 0, unroll=4)

    @pl.when(i == 0)
    def _():
        gather(0, 0)

    @pl.when(i + 1 < n)
    def _():
        gather(i + 1, 1 - slot)

    for k in range(2):
        pltpu.make_async_copy(y_ref.at[pl.ds(0, TMC)], ybuf.at[slot, k], sem.at[slot]).wait()

    r = r_ref[...]
    f = r[:, 2:3] * ybuf[slot, 0] + r[:, 3:4] * ybuf[slot, 1]
    gate = _seq_vec(gp_ref, gs_ref, i, npb, bpb, TMC)
    return x_ref[...] + gate * f


def _moe_combine_kernel(pos_ref, y_ref, x_ref, r_ref, gp_ref, gs_ref, o_ref, ybuf, sem, *, npb, bpb):
    o_ref[...] = _moe_combine_rows(pos_ref, y_ref, x_ref, r_ref, gp_ref, gs_ref, ybuf, sem, npb, bpb)


def _moe_combine_kernel_fused(pos_ref, y_ref, x_ref, r_ref, gp_ref, gs_ref, g_ref, shp_ref, scp_ref, shs_ref,
                              scs_ref, o_ref, h_ref, ybuf, sem, *, npb, bpb):
    i = pl.program_id(0)
    x_new = _moe_combine_rows(pos_ref, y_ref, x_ref, r_ref, gp_ref, gs_ref, ybuf, sem, npb, bpb)
    o_ref[...] = x_new
    sc = _seq_vec(scp_ref, scs_ref, i, npb, bpb, TMC)
    sh = _seq_vec(shp_ref, shs_ref, i, npb, bpb, TMC)
    h_ref[...] = (_rms(x_new, g_ref[...]) * (1.0 + sc) + sh).astype(h_ref.dtype)


def _moe_combine_kernel_final(pos_ref, y_ref, x_ref, r_ref, gp_ref, gs_ref, g_ref, op_ref, os_ref, ybuf, sem,
                              *, npb, bpb):
    i = pl.program_id(0)
    x_new = _moe_combine_rows(pos_ref, y_ref, x_ref, r_ref, gp_ref, gs_ref, ybuf, sem, npb, bpb)
    out = _rms(x_new, g_ref[...])

    @pl.when(i < npb)
    def _():
        op_ref[...] = out

    @pl.when(i >= npb)
    def _():
        os_ref[...] = out


def _moe_combine(y, pos2, x, route, ap, as_, gate_idx, mp, lp, nxt=None, final_g=None):
    m, d = x.shape
    bs = as_.shape[0]
    npb = mp // TMC
    in_specs = [
        pl.BlockSpec(memory_space=pl.ANY),
        pl.BlockSpec((TMC, d), lambda i, pos: (i, 0)),
        pl.BlockSpec((TMC, ROUTE_LANES), lambda i, pos: (i, 0)),
        pl.BlockSpec((8, d), lambda i, pos: (0, gate_idx)),
        pl.BlockSpec((bs, d), lambda i, pos: (0, gate_idx)),
    ]
    out_shape = jax.ShapeDtypeStruct((m, d), F32)
    out_specs = pl.BlockSpec((TMC, d), lambda i, pos: (i, 0))
    args = (pos2, y, x, route, ap, as_)
    kernel_fn = _moe_combine_kernel
    if nxt is not None:
        g_n, ap_n, as_n, dtype_n = nxt
        in_specs += [
            pl.BlockSpec((1, d), lambda i, pos: (0, 0)),
            pl.BlockSpec((8, d), lambda i, pos: (0, 0)),
            pl.BlockSpec((8, d), lambda i, pos: (0, 1)),
            pl.BlockSpec((bs, d), lambda i, pos: (0, 0)),
            pl.BlockSpec((bs, d), lambda i, pos: (0, 1)),
        ]
        out_shape = (out_shape, jax.ShapeDtypeStruct((m, d), dtype_n))
        out_specs = (out_specs, pl.BlockSpec((TMC, d), lambda i, pos: (i, 0)))
        args += (g_n, ap_n, ap_n, as_n, as_n)
        kernel_fn = _moe_combine_kernel_fused
    elif final_g is not None:
        in_specs += [pl.BlockSpec((1, d), lambda i, pos: (0, 0))]
        out_shape = (jax.ShapeDtypeStruct((mp, d), F32), jax.ShapeDtypeStruct((m - mp, d), F32))
        out_specs = (pl.BlockSpec((TMC, d), lambda i, pos: (jnp.minimum(i, npb - 1), 0)),
                     pl.BlockSpec((TMC, d), lambda i, pos: (jnp.maximum(i - npb, 0), 0)))
        args += (final_g,)
        kernel_fn = _moe_combine_kernel_final
    grid_spec = pltpu.PrefetchScalarGridSpec(
        num_scalar_prefetch=1,
        grid=(m // TMC,),
        in_specs=in_specs,
        out_specs=out_specs,
        scratch_shapes=[pltpu.VMEM((2, 2, TMC, d), F32), pltpu.SemaphoreType.DMA((2,))],
    )
    return pl.pallas_call(
        functools.partial(kernel_fn, npb=npb, bpb=lp // TMC),
        out_shape=out_shape,
        grid_spec=grid_spec,
        compiler_params=_cparams(("arbitrary",)),
        name="moe_combine",
    )(*args)


def _moe_layer(x, h2, route, w13, w2, layer, ap, as_, mp, lp, nxt=None, final_g=None):
    m = x.shape[0]
    n_blocks = (2 * m + N_EXPERTS * (MOE_BM - 1) + MOE_BM - 1) // MOE_BM
    assert n_blocks <= PLAN_ROWS and m % PLAN_TB == 0 and m % TMC == 0 and m % TMS == 0
    pos, info = _moe_plan(route)
    pos2 = pos[:, 0:2].reshape(2 * m)
    blk_e, n_used = info[:n_blocks, 0], info[0:1, 1]
    bend, nblk = info[:N_EXPERTS, 2], info[:N_EXPERTS, 3]
    seq, nxt_e = info[:n_blocks, 4], info[:n_blocks, 5]
    xs = _moe_scatter(h2, pos2, bend, nblk, n_used, n_blocks * MOE_BM)
    y = _moe_experts(xs, blk_e, n_used, seq, nxt_e, w13, w2, layer)
    return _moe_combine(y, pos2, x, route, ap, as_, 5, mp, lp, nxt=nxt, final_g=final_g)


def kernel(x_prompt, x_sample, c_prompt, c_sample, state_l0_ssm_re, state_l0_ssm_im, cache_l1_conformer_conv, cache_l2_short_conv, state_l3_ssm_re, state_l3_ssm_im, norm1_g, norm2_g, w_ada, b_ada, final_norm_g, ssm_a_re, ssm_a_im, ssm_log_dt, ssm_b_re, ssm_b_im, ssm_c_re, ssm_c_im, ssm_d, ssm_w_glu, cf_w_pw1, cf_w_dw, cf_b_dw, cf_ln_g, cf_ln_b, cf_w_pw2, sc_w_in, sc_w_conv, sc_w_out, moe_w_group, moe_b_group, moe_w_expert, moe_b_expert, moe_w13, moe_w2):
    bp, lp, d = x_prompt.shape
    bs, ls, _ = x_sample.shape
    depth = w_ada.shape[0]
    mp, ms = bp * lp, bs * ls
    g_ssm, p_ssm = ssm_a_re.shape[1:]
    assert d == D_MODEL and lp % TM == 0 and ms % TM == 0 and TM % bs == 0 and bp <= 8 and bs % 16 == 0
    assert TMC % bs == 0 and mp % ms == 0

    x = jnp.concatenate([x_prompt.reshape(mp, d), x_sample.transpose(1, 0, 2).reshape(ms, d)], axis=0)
    c_all = jnp.concatenate([c_prompt, jnp.zeros((8 - bp, d), F32), c_sample], axis=0)
    ada = _ada_all(c_all, w_ada, b_ada)
    n_pad = ROUTE_LANES - N_GROUPS - N_EXPERTS
    wr_all = jnp.concatenate([moe_w_group, moe_w_expert, jnp.zeros((depth, d, n_pad), F32)], axis=2)
    br_all = jnp.concatenate([moe_b_group, moe_b_expert, jnp.zeros((depth, n_pad), F32)], axis=1)

    sample_init = [(state_l0_ssm_re, state_l0_ssm_im), (cache_l1_conformer_conv,), (cache_l2_short_conv,),
                   (state_l3_ssm_re, state_l3_ssm_im)]
    p_states, s_states = [], []

    def mixer_in_dtype(layer):
        return F32 if layer % 3 == 0 else BF16

    h = _modulate(x, norm1_g[0:1], ada[0, 0:8], ada[0, 8:], 0, 1, mp, lp, mixer_in_dtype(0))
    for i in range(depth):
        ap, as_ = ada[i, 0:8], ada[i, 8:]
        kind, j = i % 3, i // 3
        if kind == 0:
            params = _s5_params(ssm_a_re[j], ssm_a_im[j], ssm_log_dt[j], ssm_b_re[j], ssm_b_im[j],
                                ssm_c_re[j], ssm_c_im[j])
            dsk = ssm_d[j:j + 1]
            pp, pre, pim = _s5_prompt(h, bp, lp, params, dsk)
            h0re, h0im = sample_init[i]
            psm, sre, sim = _s5_sample(h, mp, bs, ls, h0re, h0im, params, dsk)
            p_states.append((pre[:bp].reshape(bp, g_ssm, p_ssm), pim[:bp].reshape(bp, g_ssm, p_ssm)))
            s_states.append((sre, sim))
            pre_out = (pp.reshape(mp, d), psm)
            w_out = ssm_w_glu
        elif kind == 1:
            gl = _mm_split((h,), cf_w_pw1, j, 2, _mm_glu_kernel, 1, 512, mp, "mm_glu")
            extra = (cf_b_dw, cf_ln_g, cf_ln_b)
            pp = _conv_prompt(gl, bp, lp, cf_w_dw, j, "cf", extra)
            (cache,) = sample_init[i]
            g_tm = gl[mp:].reshape(ls, bs, d)
            psm = _conv_sample(g_tm, cache.transpose(1, 0, 2), cf_w_dw, j, "cf", extra)
            hist = CF_WIDTH - 1
            p_states.append((jnp.stack([gl[b * lp + lp - hist:(b + 1) * lp] for b in range(bp)]),))
            s_states.append((jnp.concatenate([cache, g_tm.transpose(1, 0, 2)], axis=1)[:, -hist:],))
            pre_out = (pp, psm.reshape(ms, d))
            w_out = cf_w_pw2
        else:
            bg, v = _mm_split((h,), sc_w_in, j, 3, _mm_bcv_kernel, 2, 512, mp, "mm_bcv")
            pp = _conv_prompt(v, bp, lp, sc_w_conv, j, "sc", (bg,))
            (cache,) = sample_init[i]
            v_tm = v[mp:].reshape(ls, bs, d)
            psm = _conv_sample(v_tm, cache.transpose(1, 0, 2), sc_w_conv, j, "sc", (bg[mp:].reshape(ls, bs, d),))
            hist = SC_WIDTH - 1
            p_states.append((jnp.stack([v[b * lp + lp - hist:(b + 1) * lp] for b in range(bp)]),))
            s_states.append((jnp.concatenate([cache, v_tm.transpose(1, 0, 2)], axis=1)[:, -hist:],))
            pre_out = (pp, psm.reshape(ms, d))
            w_out = sc_w_out

        x = _mm_resid(pre_out, w_out, j, x, ap, as_, 2, mp, lp, glu=(kind == 0))

        h2, route = _modulate(x, norm2_g[i:i + 1], ap, as_, 3, 4, mp, lp, F32, router=(wr_all[i], br_all[i:i + 1]))
        if i + 1 < depth:
            nxt = (norm1_g[i + 1:i + 2], ada[i + 1, 0:8], ada[i + 1, 8:], mixer_in_dtype(i + 1))
            x, h = _moe_layer(x, h2, route, moe_w13, moe_w2, i, ap, as_, mp, lp, nxt=nxt)
        else:
            y_p, y_s = _moe_layer(x, h2, route, moe_w13, moe_w2, i, ap, as_, mp, lp,
                                  final_g=final_norm_g.reshape(1, d))

    y_prompt = y_p.reshape(bp, lp, d)
    y_sample = y_s.reshape(ls, bs, d).transpose(1, 0, 2)
    ps, ss = p_states, s_states
    return (y_prompt, y_sample,
            ps[0][0], ps[0][1], ps[1][0], ps[2][0], ps[3][0], ps[3][1],
            ss[0][0], ss[0][1], ss[1][0], ss[2][0], ss[3][0], ss[3][1])
```
